```python
import math
import jax, jax.numpy as jnp
from jax import lax
import numpy as np

D_MODEL = 1024
BATCH = 4
SEQ = 4096
DEPTH = 4

GRID_W = 64
CTX_LEN = 256
N_MIXERS = 3
N_LAYERS_A = (DEPTH + 2) // 3
N_LAYERS_B = (DEPTH + 1) // 3
N_LAYERS_C = DEPTH // 3

A_HEADS = 8
A_KV_HEADS = 2
A_HEAD_DIM = 128
B_HEADS = 16
B_HEAD_DIM = D_MODEL // B_HEADS
B_MAX_KH = 8
B_KW = 16
C_HEADS = 8
C_NOPE = 128
C_ROPE = 64
C_V = 128
C_Q_RANK = 384
C_KV_RANK = 256
D_FF = -(-8 * D_MODEL // (3 * 256)) * 256

ROPE_BASE = 10000.0
Q_BLOCK = 128
RMS_EPS = 1e-6
LN_EPS = 1e-5
DEEPNORM_ALPHA = (2.0 * DEPTH) ** 0.25
DEEPNORM_BETA = (8.0 * DEPTH) ** -0.25

kernel_name = "hybrid_interleaved_gqa_natten_mla_deepnorm"


def rms_norm(t, g):
    tf = t.astype(jnp.float32)
    y = tf * lax.rsqrt(jnp.mean(tf * tf, axis=-1, keepdims=True) + RMS_EPS)
    return (y * g.astype(jnp.float32)).astype(t.dtype)


def layer_norm(t, g, b):
    tf = t.astype(jnp.float32)
    mu = jnp.mean(tf, axis=-1, keepdims=True)
    var = jnp.mean(jnp.square(tf - mu), axis=-1, keepdims=True)
    y = (tf - mu) * lax.rsqrt(var + LN_EPS)
    return (y * g.astype(jnp.float32) + b.astype(jnp.float32)).astype(t.dtype)


def modulate(t, shift, scale):
    return t * (1 + scale) + shift


def rope_1d(t, ang):
    cos = jnp.cos(ang)[None, :, None, :].astype(t.dtype)
    sin = jnp.sin(ang)[None, :, None, :].astype(t.dtype)
    t1, t2 = jnp.split(t, 2, axis=-1)
    return jnp.concatenate([t1 * cos - t2 * sin, t2 * cos + t1 * sin], axis=-1)


def axial_rope(t, pos_r, pos_c):
    half = t.shape[-1] // 2
    freqs = ROPE_BASE ** (-jnp.arange(0, half, 2, dtype=jnp.float32) / half)
    ang_r = pos_r.astype(jnp.float32)[:, None] * freqs[None, :]
    ang_c = pos_c.astype(jnp.float32)[:, None] * freqs[None, :]
    t_r, t_c = jnp.split(t, 2, axis=-1)
    return jnp.concatenate([rope_1d(t_r, ang_r), rope_1d(t_c, ang_c)], axis=-1)


def softmax_attend(q, keys, vals, scale):
    s = jnp.concatenate([jnp.einsum('bqgrd,bkgd->bgrqk', q, k) for k in keys], axis=-1)
    p = jax.nn.softmax(s.astype(jnp.float32) * scale, axis=-1).astype(q.dtype)
    offs = np.cumsum([0] + [k.shape[1] for k in keys])
    out = None
    for j, vj in enumerate(vals):
        pj = p[..., int(offs[j]):int(offs[j + 1])]
        oj = jnp.einsum('bgrqk,bkgd->bqgrd', pj, vj)
        out = oj if out is None else out + oj
    return out


def blocked_attention(q, k, v, kc, vc, scale):
    B, N, G, R, dk = q.shape
    nb = N // Q_BLOCK
    qb = q.reshape(B, nb, Q_BLOCK, G, R, dk).transpose(1, 0, 2, 3, 4, 5)
    ob = lax.map(lambda qi: softmax_attend(qi, [k, kc], [v, vc], scale), qb)
    return ob.transpose(1, 0, 2, 3, 4, 5).reshape(B, N, G, R, v.shape[-1])


def swiglu(t, w_gate, w_up, w_down):
    return (jax.nn.silu(t @ w_gate) * (t @ w_up)) @ w_down


def mixer_gqa(h, hc, w_qkv, q_gain, k_gain, w_o, pos_r, pos_c, need_ctx):
    B, N, _ = h.shape
    L = hc.shape[1]
    rep = A_HEADS // A_KV_HEADS
    split = [A_HEADS * A_HEAD_DIM, (A_HEADS + A_KV_HEADS) * A_HEAD_DIM]

    def proj(t):
        bt, tt, _ = t.shape
        q, k, v = jnp.split(t @ w_qkv, split, axis=-1)
        q = rms_norm(q.reshape(bt, tt, A_HEADS, A_HEAD_DIM), q_gain)
        k = rms_norm(k.reshape(bt, tt, A_KV_HEADS, A_HEAD_DIM), k_gain)
        v = v.reshape(bt, tt, A_KV_HEADS, A_HEAD_DIM)
        return q, k, v

    q, k, v = proj(h)
    qc, kc, vc = proj(hc)
    q = axial_rope(q, pos_r, pos_c)
    k = axial_rope(k, pos_r, pos_c)
    scale = A_HEAD_DIM ** -0.5
    o = blocked_attention(q.reshape(B, N, A_KV_HEADS, rep, A_HEAD_DIM), k, v, kc, vc, scale)
    y = o.reshape(B, N, A_HEADS * A_HEAD_DIM) @ w_o
    yc = None
    if need_ctx:
        oc = softmax_attend(qc.reshape(B, L, A_KV_HEADS, rep, A_HEAD_DIM), [kc], [vc], scale)
        yc = oc.reshape(B, L, A_HEADS * A_HEAD_DIM) @ w_o
    return y, yc


def mixer_neighbourhood(h, hc, w_qkv, rpb, w_o, need_ctx):
    B, N, _ = h.shape
    L = hc.shape[1]
    rows = N // GRID_W
    kh = min(B_MAX_KH, rows)
    scale = B_HEAD_DIM ** -0.5

    def proj(t):
        bt, tt, _ = t.shape
        q, k, v = jnp.split(t @ w_qkv, 3, axis=-1)
        shp = (bt, tt, B_HEADS, B_HEAD_DIM)
        return q.reshape(shp), k.reshape(shp), v.reshape(shp)

    q, k, v = proj(h)
    qc, kc, vc = proj(hc)
    grid = (B, rows, GRID_W, B_HEADS, B_HEAD_DIM)
    k = k.reshape(grid)
    v = v.reshape(grid)
    q_rows = q.reshape(grid).transpose(1, 0, 2, 3, 4)

    col = jnp.arange(GRID_W)
    col_start = jnp.clip(col - B_KW // 2, 0, GRID_W - B_KW)
    col_idx = col_start[:, None] + jnp.arange(B_KW)[None, :]
    dc = col_idx - col[:, None] + (B_KW - 1)
    bias_c = rpb[:, :, dc]

    def row_block(args):
        r, qr = args
        rs = jnp.clip(r - kh // 2, 0, rows - kh)
        k_rows = lax.dynamic_slice_in_dim(k, rs, kh, axis=1)
        v_rows = lax.dynamic_slice_in_dim(v, rs, kh, axis=1)
        k_win = k_rows[:, :, col_idx]
        v_win = v_rows[:, :, col_idx]
        dr = rs + jnp.arange(kh) - r + (B_MAX_KH - 1)
        bias = jnp.take(bias_c, dr, axis=1).transpose(0, 2, 1, 3)
        s_loc = (jnp.einsum('bwhd,biwjhd->bhwij', qr, k_win).astype(jnp.float32) * scale
                 + bias[None].astype(jnp.float32)).reshape(B, B_HEADS, GRID_W, kh * B_KW)
        s_ctx = jnp.einsum('bwhd,blhd->bhwl', qr, kc).astype(jnp.float32) * scale
        p = jax.nn.softmax(jnp.concatenate([s_loc, s_ctx], axis=-1), axis=-1).astype(qr.dtype)
        p_loc = p[..., :kh * B_KW].reshape(B, B_HEADS, GRID_W, kh, B_KW)
        p_ctx = p[..., kh * B_KW:]
        return (jnp.einsum('bhwij,biwjhd->bwhd', p_loc, v_win)
                + jnp.einsum('bhwl,blhd->bwhd', p_ctx, vc))

    o_rows = lax.map(row_block, (jnp.arange(rows), q_rows))
    y = o_rows.transpose(1, 0, 2, 3, 4).reshape(B, N, D_MODEL) @ w_o
    yc = None
    if need_ctx:
        oc = softmax_attend(qc[:, :, :, None, :], [kc], [vc], scale)
        yc = oc.reshape(B, L, D_MODEL) @ w_o
    return y, yc


def mixer_mla(h, hc, w_dqkv, q_a_gain, kv_a_gain, w_uq, w_ukv, w_o, pos_r, pos_c, need_ctx):
    B, N, _ = h.shape
    L = hc.shape[1]
    scale = (C_NOPE + C_ROPE) ** -0.5

    def proj(t, rotate):
        bt, tt, _ = t.shape
        q_lat, kv_lat, k_rope = jnp.split(t @ w_dqkv, [C_Q_RANK, C_Q_RANK + C_KV_RANK], axis=-1)
        q = (rms_norm(q_lat, q_a_gain) @ w_uq).reshape(bt, tt, C_HEADS, C_NOPE + C_ROPE)
        kv = (rms_norm(kv_lat, kv_a_gain) @ w_ukv).reshape(bt, tt, C_HEADS, C_NOPE + C_V)
        q_nope, q_rope = jnp.split(q, [C_NOPE], axis=-1)
        k_nope, v = jnp.split(kv, [C_NOPE], axis=-1)
        k_rope = k_rope[:, :, None, :]
        if rotate:
            q_rope = axial_rope(q_rope, pos_r, pos_c)
            k_rope = axial_rope(k_rope, pos_r, pos_c)
        q = jnp.concatenate([q_nope, q_rope], axis=-1)
        k = jnp.concatenate([k_nope, jnp.broadcast_to(k_rope, (bt, tt, C_HEADS, C_ROPE))], axis=-1)
        return q[:, :, :, None, :], k, v

    q, k, v = proj(h, True)
    qc, kc, vc = proj(hc, False)
    o = blocked_attention(q, k, v, kc, vc, scale)
    y = o.reshape(B, N, C_HEADS * C_V) @ w_o
    yc = None
    if need_ctx:
        oc = softmax_attend(qc, [kc], [vc], scale)
        yc = oc.reshape(B, L, C_HEADS * C_V) @ w_o
    return y, yc


def setup_inputs(seed: int = 0) -> dict:
    key = jax.random.key(seed)
    ks = iter(jax.random.split(key, 32))

    def nrm(shape, scale):
        return jax.random.normal(next(ks), shape, jnp.float32) * scale

    def gain(shape):
        return 1.0 + nrm(shape, 0.02)

    D = D_MODEL
    inp = {}
    inp["x"] = nrm((BATCH, SEQ, D), 1.0)
    inp["c"] = nrm((BATCH, D), 1.0)
    inp["ctx"] = nrm((BATCH, CTX_LEN, D), 1.0)
    inp["c_ctx"] = nrm((D,), 1.0)
    inp["w_ada"] = nrm((DEPTH, D, 6 * D), 0.5 * D ** -0.5)
    inp["b_ada"] = nrm((DEPTH, 6 * D), 0.02)
    inp["ln_g"] = gain((DEPTH, 2, D))
    inp["ln_b"] = nrm((DEPTH, 2, D), 0.02)
    inp["w_ffn_gate"] = nrm((DEPTH, D, D_FF), D ** -0.5)
    inp["w_ffn_up"] = nrm((DEPTH, D, D_FF), D ** -0.5)
    inp["w_ffn_down"] = nrm((DEPTH, D_FF, D), DEEPNORM_BETA * D_FF ** -0.5)
    inp["a_w_qkv"] = nrm((N_LAYERS_A, D, (A_HEADS + 2 * A_KV_HEADS) * A_HEAD_DIM), D ** -0.5)
    inp["a_q_gain"] = gain((N_LAYERS_A, A_HEAD_DIM))
    inp["a_k_gain"] = gain((N_LAYERS_A, A_HEAD_DIM))
    inp["a_w_o"] = nrm((N_LAYERS_A, A_HEADS * A_HEAD_DIM, D), DEEPNORM_BETA * (A_HEADS * A_HEAD_DIM) ** -0.5)
    inp["b_w_qkv"] = nrm((N_LAYERS_B, D, 3 * D), D ** -0.5)
    inp["b_rpb"] = nrm((N_LAYERS_B, B_HEADS, 2 * B_MAX_KH - 1, 2 * B_KW - 1), 0.5)
    inp["b_w_o"] = nrm((N_LAYERS_B, D, D), DEEPNORM_BETA * D ** -0.5)
    inp["c_w_dqkv"] = nrm((N_LAYERS_C, D, C_Q_RANK + C_KV_RANK + C_ROPE), D ** -0.5)
    inp["c_q_a_gain"] = gain((N_LAYERS_C, C_Q_RANK))
    inp["c_kv_a_gain"] = gain((N_LAYERS_C, C_KV_RANK))
    inp["c_w_uq"] = nrm((N_LAYERS_C, C_Q_RANK, C_HEADS * (C_NOPE + C_ROPE)), C_Q_RANK ** -0.5)
    inp["c_w_ukv"] = nrm((N_LAYERS_C, C_KV_RANK, C_HEADS * (C_NOPE + C_V)), C_KV_RANK ** -0.5)
    inp["c_w_o"] = nrm((N_LAYERS_C, C_HEADS * C_V, D), DEEPNORM_BETA * (C_HEADS * C_V) ** -0.5)
    return inp


def reference(x, c, ctx, c_ctx, w_ada, b_ada, ln_g, ln_b, w_ffn_gate, w_ffn_up, w_ffn_down,
              a_w_qkv, a_q_gain, a_k_gain, a_w_o,
              b_w_qkv, b_rpb, b_w_o,
              c_w_dqkv, c_q_a_gain, c_kv_a_gain, c_w_uq, c_w_ukv, c_w_o):
    N = x.shape[1]
    t = jnp.arange(N)
    pos_r = t // GRID_W
    pos_c = t % GRID_W
    cond_lat = jax.nn.silu(c)
    cond_ctx = jax.nn.silu(c_ctx)
    xc = ctx
    for i in range(DEPTH):
        need_ctx = i < DEPTH - 1
        mod = cond_lat @ w_ada[i] + b_ada[i]
        mod_c = cond_ctx @ w_ada[i] + b_ada[i]
        sh1, sc1, g1, sh2, sc2, g2 = jnp.split(mod[:, None, :], 6, axis=-1)
        sh1c, sc1c, g1c, sh2c, sc2c, g2c = jnp.split(mod_c, 6, axis=-1)
        h = modulate(x, sh1, sc1)
        hc = modulate(xc, sh1c, sc1c)
        kind, j = i % N_MIXERS, i // N_MIXERS
        if kind == 0:
            y, yc = mixer_gqa(h, hc, a_w_qkv[j], a_q_gain[j], a_k_gain[j], a_w_o[j],
                              pos_r, pos_c, need_ctx)
        elif kind == 1:
            y, yc = mixer_neighbourhood(h, hc, b_w_qkv[j], b_rpb[j], b_w_o[j], need_ctx)
        else:
            y, yc = mixer_mla(h, hc, c_w_dqkv[j], c_q_a_gain[j], c_kv_a_gain[j], c_w_uq[j],
                              c_w_ukv[j], c_w_o[j], pos_r, pos_c, need_ctx)
        x = layer_norm(DEEPNORM_ALPHA * x + g1 * y, ln_g[i, 0], ln_b[i, 0])
        f = swiglu(modulate(x, sh2, sc2), w_ffn_gate[i], w_ffn_up[i], w_ffn_down[i])
        x = layer_norm(DEEPNORM_ALPHA * x + g2 * f, ln_g[i, 1], ln_b[i, 1])
        if need_ctx:
            xc = layer_norm(DEEPNORM_ALPHA * xc + g1c * yc, ln_g[i, 0], ln_b[i, 0])
            fc = swiglu(modulate(xc, sh2c, sc2c), w_ffn_gate[i], w_ffn_up[i], w_ffn_down[i])
            xc = layer_norm(DEEPNORM_ALPHA * xc + g2c * fc, ln_g[i, 1], ln_b[i, 1])
    return x
```

```python
import functools
import math

import numpy as np
import jax
import jax.numpy as jnp
from jax import lax
from jax.experimental import pallas as pl
from jax.experimental.pallas import tpu as pltpu

GRID_W = 64
A_HEADS, A_KV_HEADS, A_HEAD_DIM = 8, 2, 128
B_HEADS, B_HEAD_DIM, B_MAX_KH, B_KW = 16, 64, 8, 16
C_HEADS, C_NOPE, C_ROPE, C_V, C_Q_RANK, C_KV_RANK = 8, 128, 64, 128, 384, 256
ROPE_BASE = 10000.0
RMS_EPS = 1e-6
LN_EPS = 1e-5
LOG2E = math.log2(math.e)
NEG_BIG = -1e30

LANES = 128
V7X_VMEM_BYTES = 64 * 1024 * 1024
VMEM_LIMIT = 56 * 1024 * 1024

ROW_TILE = 256
MOD_ROWS = 8
NA_Q_ROWS = 4
NA_K_ROWS = 12

BF16 = jnp.bfloat16
F32 = jnp.float32


def _cparams(n_axes, vmem=None):
    return pltpu.CompilerParams(dimension_semantics=("arbitrary",) * n_axes, vmem_limit_bytes=vmem)


def _resident(shape, index_map):
    return pl.BlockSpec(shape, index_map, pipeline_mode=pl.Buffered(1))


def _ada_kernel(c_ref, w_ref, b_ref, o_ref):
    c = c_ref[...]
    cs = c / (1.0 + jnp.exp(-c))
    o_ref[...] = jnp.dot(cs, w_ref[...], preferred_element_type=F32) + b_ref[...]


def _ada_call(cond, w_ada, b_ada):
    depth, d, n6 = w_ada.shape
    tn = n6 // 4
    return pl.pallas_call(
        _ada_kernel,
        grid=(depth, n6 // tn),
        in_specs=[
            pl.BlockSpec((MOD_ROWS, d), lambda l, j: (0, 0)),
            pl.BlockSpec((None, d, tn), lambda l, j: (l, 0, j)),
            pl.BlockSpec((None, 1, tn), lambda l, j: (l, 0, j)),
        ],
        out_specs=pl.BlockSpec((None, MOD_ROWS, tn), lambda l, j: (l, 0, j)),
        out_shape=jax.ShapeDtypeStruct((depth, MOD_ROWS, n6), F32),
        compiler_params=_cparams(2, 40 * 1024 * 1024),
        name="ada_mod",
    )(cond, w_ada, b_ada.reshape(depth, 1, n6))


def _mod_row(n_lat_tiles, ctx_row):
    b, i = pl.program_id(0), pl.program_id(1)
    return jnp.where(i < n_lat_tiles, b, ctx_row)


def _mod_vec(mod_ref, row, k, d):
    return mod_ref[pl.ds(row, 1), k * d:(k + 1) * d]


def _modulated(x_ref, mod_ref, row, k_shift, d):
    sh = _mod_vec(mod_ref, row, k_shift, d)
    sc = _mod_vec(mod_ref, row, k_shift + 1, d)
    return (x_ref[...] * (1.0 + sc) + sh).astype(BF16)


def _rms(t, g):
    return t * lax.rsqrt(jnp.mean(t * t, axis=-1, keepdims=True) + RMS_EPS) * g


def _rope(t, cos, sin):
    return t * cos + pltpu.roll(t, LANES // 2, 1) * sin


def _layer_norm(t, g, b):
    mu = jnp.mean(t, axis=-1, keepdims=True)
    c = t - mu
    var = jnp.mean(c * c, axis=-1, keepdims=True)
    return c * lax.rsqrt(var + LN_EPS) * g + b


def _qkv_a_kernel(x_ref, mod_ref, w_ref, gq_ref, gk_ref, cos_ref, sin_ref, o_ref, *, n_lat_tiles, ctx_row, q_scale):
    d = x_ref.shape[-1]
    row = _mod_row(n_lat_tiles, ctx_row)
    h = _modulated(x_ref, mod_ref, row, 0, d)
    acc = jnp.dot(h, w_ref[...], preferred_element_type=F32)
    cos, sin = cos_ref[...], sin_ref[...]
    gq = gq_ref[...] * q_scale
    gk = gk_ref[...]
    hd = A_HEAD_DIM
    for j in range(A_HEADS + A_KV_HEADS):
        t = acc[:, j * hd:(j + 1) * hd]
        y = _rope(_rms(t, gq if j < A_HEADS else gk), cos, sin)
        o_ref[:, j * hd:(j + 1) * hd] = y.astype(BF16)
    v0 = (A_HEADS + A_KV_HEADS) * hd
    o_ref[:, v0:] = acc[:, v0:].astype(BF16)


def _qkv_a_call(xa, mod, layer, w, gq, gk, cos, sin, n_lat):
    bsz, t, d = xa.shape
    n_out = w.shape[1]
    tm = ROW_TILE
    kern = functools.partial(_qkv_a_kernel, n_lat_tiles=n_lat // tm, ctx_row=bsz,
                             q_scale=A_HEAD_DIM ** -0.5 * LOG2E)
    return pl.pallas_call(
        kern,
        grid=(bsz, t // tm),
        in_specs=[
            pl.BlockSpec((None, tm, d), lambda b, i: (b, i, 0)),
            pl.BlockSpec((None, MOD_ROWS, 6 * d), lambda b, i: (layer, 0, 0)),
            _resident((d, n_out), lambda b, i: (0, 0)),
            pl.BlockSpec((1, A_HEAD_DIM), lambda b, i: (0, 0)),
            pl.BlockSpec((1, A_HEAD_DIM), lambda b, i: (0, 0)),
            pl.BlockSpec((tm, LANES), lambda b, i: (i, 0)),
            pl.BlockSpec((tm, LANES), lambda b, i: (i, 0)),
        ],
        out_specs=pl.BlockSpec((None, tm, n_out), lambda b, i: (b, i, 0)),
        out_shape=jax.ShapeDtypeStruct((bsz, t, n_out), BF16),
        compiler_params=_cparams(2, 32 * 1024 * 1024),
        name="qkv_gqa",
    )(xa, mod, w, gq, gk, cos, sin)


def _qkv_b_kernel(x_ref, mod_ref, w_ref, o_ref, *, n_lat_tiles, ctx_row, q_scale):
    d = x_ref.shape[-1]
    row = _mod_row(n_lat_tiles, ctx_row)
    h = _modulated(x_ref, mod_ref, row, 0, d)
    acc = jnp.dot(h, w_ref[...], preferred_element_type=F32)
    o_ref[:, :d] = (acc[:, :d] * q_scale).astype(BF16)
    o_ref[:, d:] = acc[:, d:].astype(BF16)


def _qkv_b_call(xa, mod, layer, w, n_lat):
    bsz, t, d = xa.shape
    n_out = w.shape[1]
    tm = ROW_TILE
    kern = functools.partial(_qkv_b_kernel, n_lat_tiles=n_lat // tm, ctx_row=bsz,
                             q_scale=B_HEAD_DIM ** -0.5 * LOG2E)
    return pl.pallas_call(
        kern,
        grid=(bsz, t // tm),
        in_specs=[
            pl.BlockSpec((None, tm, d), lambda b, i: (b, i, 0)),
            pl.BlockSpec((None, MOD_ROWS, 6 * d), lambda b, i: (layer, 0, 0)),
            _resident((d, n_out), lambda b, i: (0, 0)),
        ],
        out_specs=pl.BlockSpec((None, tm, n_out), lambda b, i: (b, i, 0)),
        out_shape=jax.ShapeDtypeStruct((bsz, t, n_out), BF16),
        compiler_params=_cparams(2, 40 * 1024 * 1024),
        name="qkv_na",
    )(xa, mod, w)


def _proj_c_kernel(x_ref, mod_ref, wd_ref, gq_ref, gkv_ref, wuq_ref, wukv_ref, cos_ref, sin_ref,
                   q_ref, k_ref, v_ref, *, n_lat_tiles, ctx_row, q_scale):
    d = x_ref.shape[-1]
    row = _mod_row(n_lat_tiles, ctx_row)
    h = _modulated(x_ref, mod_ref, row, 0, d)
    cos, sin = cos_ref[...], sin_ref[...]
    dn = jnp.dot(h, wd_ref[...], preferred_element_type=F32)
    r0, r1 = C_Q_RANK, C_Q_RANK + C_KV_RANK
    ql = _rms(dn[:, :r0], gq_ref[...]).astype(BF16)
    kvl = _rms(dn[:, r0:r1], gkv_ref[...]).astype(BF16)
    kr = _rope(dn[:, r1:], cos, sin).astype(BF16)
    q = jnp.dot(ql, wuq_ref[...], preferred_element_type=F32)
    kv = jnp.dot(kvl, wukv_ref[...], preferred_element_type=F32)
    w2 = 2 * LANES
    for hh in range(C_HEADS):
        q_ref[:, hh * w2:hh * w2 + LANES] = (q[:, hh * w2:hh * w2 + LANES] * q_scale).astype(BF16)
        qr = _rope(q[:, hh * w2 + LANES:(hh + 1) * w2], cos, sin) * q_scale
        q_ref[:, hh * w2 + LANES:(hh + 1) * w2] = qr.astype(BF16)
        k_ref[:, hh * w2:hh * w2 + LANES] = kv[:, hh * w2:hh * w2 + LANES].astype(BF16)
        k_ref[:, hh * w2 + LANES:(hh + 1) * w2] = kr
        v_ref[:, hh * LANES:(hh + 1) * LANES] = kv[:, hh * w2 + LANES:(hh + 1) * w2].astype(BF16)


def _proj_c_call(xa, mod, layer, wd, gq, gkv, wuq, wukv, cos, sin, n_lat):
    bsz, t, d = xa.shape
    tm = ROW_TILE
    kern = functools.partial(_proj_c_kernel, n_lat_tiles=n_lat // tm, ctx_row=bsz,
                             q_scale=(C_NOPE + C_ROPE) ** -0.5 * LOG2E)
    const = lambda b, i: (0, 0)
    qk_w = C_HEADS * 2 * LANES
    return pl.pallas_call(
        kern,
        grid=(bsz, t // tm),
        in_specs=[
            pl.BlockSpec((None, tm, d), lambda b, i: (b, i, 0)),
            pl.BlockSpec((None, MOD_ROWS, 6 * d), lambda b, i: (layer, 0, 0)),
            _resident(wd.shape, const),
            pl.BlockSpec(gq.shape, const),
            pl.BlockSpec(gkv.shape, const),
            _resident(wuq.shape, const),
            _resident(wukv.shape, const),
            pl.BlockSpec((tm, LANES), lambda b, i: (i, 0)),
            pl.BlockSpec((tm, LANES), lambda b, i: (i, 0)),
        ],
        out_specs=[
            pl.BlockSpec((None, tm, qk_w), lambda b, i: (b, i, 0)),
            pl.BlockSpec((None, tm, qk_w), lambda b, i: (b, i, 0)),
            pl.BlockSpec((None, tm, C_HEADS * C_V), lambda b, i: (b, i, 0)),
        ],
        out_shape=[
            jax.ShapeDtypeStruct((bsz, t, qk_w), BF16),
            jax.ShapeDtypeStruct((bsz, t, qk_w), BF16),
            jax.ShapeDtypeStruct((bsz, t, C_HEADS * C_V), BF16),
        ],
        compiler_params=_cparams(2, 40 * 1024 * 1024),
        name="proj_mla",
    )(xa, mod, wd, gq, gkv, wuq, wukv, cos, sin)


def _pair_masks(shape):
    lane = lax.broadcasted_iota(jnp.int32, shape, 1)
    return lane < (LANES // 2)


def _stack_queries(q_ref, mode, n_heads, dk):
    if mode == "pair":
        qb = q_ref[...]
        lo = _pair_masks(qb.shape)
        zero = jnp.zeros_like(qb)
        return jnp.concatenate([jnp.where(lo, qb, zero), jnp.where(lo, zero, qb)], axis=0)
    if n_heads == 1:
        return q_ref[...]
    return jnp.concatenate([q_ref[:, r * dk:(r + 1) * dk] for r in range(n_heads)], axis=0)


def _unstack_outputs(o, o_ref, mode, n_heads, dv, tq):
    if mode == "pair":
        lo = _pair_masks((tq, LANES))
        o_ref[...] = jnp.where(lo, o[:tq], o[tq:]).astype(o_ref.dtype)
    else:
        for r in range(n_heads):
            o_ref[:, r * dv:(r + 1) * dv] = o[r * tq:(r + 1) * tq].astype(o_ref.dtype)


def _dot_nt(a, b):
    return lax.dot_general(a, b, (((1,), (1,)), ((), ())), preferred_element_type=F32)


def _flash_kernel(q_ref, k_ref, v_ref, *rest, mode, n_heads, dk, dv, chunks):
    o_ref = rest[-1]
    tq = q_ref.shape[0]
    q = _stack_queries(q_ref, mode, n_heads, dk)
    m_rows = q.shape[0]

    def step(start, size, carry):
        m, l, acc = carry
        k = k_ref[pl.ds(start, size), :]
        v = v_ref[pl.ds(start, size), :]
        s = _dot_nt(q, k)
        m_new = jnp.maximum(m, jnp.max(s, axis=-1, keepdims=True))
        alpha = jnp.exp2(m - m_new)
        p = jnp.exp2(s - m_new)
        l = alpha * l + jnp.sum(p, axis=-1, keepdims=True)
        acc = alpha * acc + jnp.dot(p.astype(BF16), v, preferred_element_type=F32)
        return m_new, l, acc

    carry = (jnp.full((m_rows, 1), NEG_BIG, F32), jnp.zeros((m_rows, 1), F32), jnp.zeros((m_rows, dv), F32))
    for start, size, count in chunks:
        if count == 1:
            carry = step(start, size, carry)
        else:
            carry = lax.fori_loop(
                0, count, lambda c, cr: step(pl.multiple_of(start + c * size, size), size, cr), carry)
    _, l, acc = carry
    _unstack_outputs(acc / l, o_ref, mode, n_heads, dv, tq)


def _flash_call(q_arr, k_arr, v_arr, *, name, mode, n_groups, n_heads, dk, dv, tq, q_tile0, n_q_tiles,
                q_col0, k_col0, v_col0, chunks, out_rows, prev=None):
    bsz, t, _ = k_arr.shape
    q_w = LANES if mode == "pair" else n_heads * dk
    o_w = LANES if mode == "pair" else n_heads * dv
    kern = functools.partial(_flash_kernel, mode=mode, n_heads=n_heads, dk=dk, dv=dv, chunks=chunks)
    in_specs = [
        pl.BlockSpec((None, tq, q_w), lambda b, g, i: (b, i + q_tile0, q_col0 + g)),
        pl.BlockSpec((None, t, dk), lambda b, g, i: (b, 0, k_col0 + g)),
        pl.BlockSpec((None, t, dv), lambda b, g, i: (b, 0, v_col0 + g)),
    ]
    args = [q_arr, k_arr, v_arr]
    aliases = {}
    if prev is not None:
        in_specs.append(pl.BlockSpec(memory_space=pl.ANY))
        args.append(prev)
        aliases = {3: 0}
    return pl.pallas_call(
        kern,
        grid=(bsz, n_groups, n_q_tiles),
        in_specs=in_specs,
        out_specs=pl.BlockSpec((None, tq, o_w), lambda b, g, i: (b, i + q_tile0, g)),
        out_shape=jax.ShapeDtypeStruct((bsz, out_rows, n_groups * o_w), BF16),
        input_output_aliases=aliases,
        compiler_params=_cparams(3, 40 * 1024 * 1024),
        name=name,
    )(*args)


def _na_kernel(q_ref, k_ref, v_ref, bias_ref, o_ref, *, n_lat, n_ctx, n_groups):
    gi = pl.program_id(2)
    rows = n_groups * NA_Q_ROWS
    tq = NA_Q_ROWS * GRID_W
    tk = NA_K_ROWS * GRID_W
    start = jnp.clip(gi * NA_Q_ROWS - B_MAX_KH // 2, 0, rows - NA_K_ROWS) * GRID_W
    start = pl.multiple_of(start, GRID_W)
    q = _stack_queries(q_ref, "pair", 2, LANES)
    kw = k_ref[pl.ds(start, tk), :]
    vw = v_ref[pl.ds(start, tk), :]
    kc = k_ref[n_lat:n_lat + n_ctx, :]
    vc = v_ref[n_lat:n_lat + n_ctx, :]
    s_w = _dot_nt(q, kw) + bias_ref[...].reshape(2 * tq, tk)
    s_c = _dot_nt(q, kc)
    m = jnp.maximum(jnp.max(s_w, axis=-1, keepdims=True), jnp.max(s_c, axis=-1, keepdims=True))
    p_w = jnp.exp2(s_w - m)
    p_c = jnp.exp2(s_c - m)
    l = jnp.sum(p_w, axis=-1, keepdims=True) + jnp.sum(p_c, axis=-1, keepdims=True)
    o = (jnp.dot(p_w.astype(BF16), vw, preferred_element_type=F32)
         + jnp.dot(p_c.astype(BF16), vc, preferred_element_type=F32))
    _unstack_outputs(o / l, o_ref, "pair", 2, LANES, tq)


def _na_call(qkv, bias, n_lat, n_ctx, out_rows):
    bsz, t, w3 = qkv.shape
    d = w3 // 3
    n_pairs = d // LANES
    tq = NA_Q_ROWS * GRID_W
    tk = NA_K_ROWS * GRID_W
    n_groups = n_lat // tq
    kern = functools.partial(_na_kernel, n_lat=n_lat, n_ctx=n_ctx, n_groups=n_groups)

    def bias_map(b, p, g):
        case = jnp.where(g == 0, 0, jnp.where(g == n_groups - 1, 2, 1))
        return (p, case, 0, 0)

    return pl.pallas_call(
        kern,
        grid=(bsz, n_pairs, n_groups),
        in_specs=[
            pl.BlockSpec((None, tq, LANES), lambda b, p, g: (b, g, p)),
            pl.BlockSpec((None, t, LANES), lambda b, p, g: (b, 0, n_pairs + p)),
            pl.BlockSpec((None, t, LANES), lambda b, p, g: (b, 0, 2 * n_pairs + p)),
            pl.BlockSpec((2, None, tq, tk), bias_map),
        ],
        out_specs=pl.BlockSpec((None, tq, LANES), lambda b, p, g: (b, g, p)),
        out_shape=jax.ShapeDtypeStruct((bsz, out_rows, d), BF16),
        compiler_params=_cparams(3, 40 * 1024 * 1024),
        name="attn_na",
    )(qkv, qkv, qkv, bias)


def _post_kernel(o_ref, x_ref, mod_ref, wo_ref, wg_ref, wu_ref, wd_ref, lng_ref, lnb_ref, out_ref,
                 *, n_lat_tiles, ctx_row, alpha, ff_chunk):
    d = x_ref.shape[-1]
    row = _mod_row(n_lat_tiles, ctx_row)
    g1 = _mod_vec(mod_ref, row, 2, d)
    sh2 = _mod_vec(mod_ref, row, 3, d)
    sc2 = _mod_vec(mod_ref, row, 4, d)
    g2 = _mod_vec(mod_ref, row, 5, d)
    y = jnp.dot(o_ref[...], wo_ref[...], preferred_element_type=F32)
    x1 = _layer_norm(alpha * x_ref[...] + g1 * y, lng_ref[0:1, :], lnb_ref[0:1, :])
    h = (x1 * (1.0 + sc2) + sh2).astype(BF16)
    d_ff = wg_ref.shape[1]
    f = jnp.zeros(x1.shape, F32)
    for c0 in range(0, d_ff, ff_chunk):
        gate = jnp.dot(h, wg_ref[:, c0:c0 + ff_chunk], preferred_element_type=F32)
        up = jnp.dot(h, wu_ref[:, c0:c0 + ff_chunk], preferred_element_type=F32)
        a = (gate / (1.0 + jnp.exp(-gate)) * up).astype(BF16)
        f = f + jnp.dot(a, wd_ref[c0:c0 + ff_chunk, :], preferred_element_type=F32)
    out_ref[...] = _layer_norm(alpha * x1 + g2 * f, lng_ref[1:2, :], lnb_ref[1:2, :])


def _post_call(o, xa, mod, layer, wo, wg, wu, wd, lng, lnb, n_lat, out_rows, alpha):
    bsz, _, d = xa.shape
    tm = ROW_TILE
    d_ff = wg.shape[1]
    kern = functools.partial(_post_kernel, n_lat_tiles=n_lat // tm, ctx_row=bsz, alpha=alpha, ff_chunk=256)
    const = lambda b, i: (0, 0)
    return pl.pallas_call(
        kern,
        grid=(bsz, out_rows // tm),
        in_specs=[
            pl.BlockSpec((None, tm, o.shape[-1]), lambda b, i: (b, i, 0)),
            pl.BlockSpec((None, tm, d), lambda b, i: (b, i, 0)),
            pl.BlockSpec((None, MOD_ROWS, 6 * d), lambda b, i: (layer, 0, 0)),
            _resident(wo.shape, const),
            _resident((d, d_ff), const),
            _resident((d, d_ff), const),
            _resident((d_ff, d), const),
            pl.BlockSpec((None, 2, d), lambda b, i: (layer, 0, 0)),
            pl.BlockSpec((None, 2, d), lambda b, i: (layer, 0, 0)),
        ],
        out_specs=pl.BlockSpec((None, tm, d), lambda b, i: (b, i, 0)),
        out_shape=jax.ShapeDtypeStruct((bsz, out_rows, d), F32),
        compiler_params=_cparams(2, VMEM_LIMIT),
        name="post_ffn",
    )(o, xa, mod, wo, wg, wu, wd, lng, lnb)


def _rope_tables(n_lat, n_ctx, n_freq):
    half = LANES // 2
    t = np.arange(n_lat)
    freqs = ROPE_BASE ** (-np.arange(0, 2 * n_freq, 2, dtype=np.float64) / (2 * n_freq))
    ang = np.concatenate([(t // GRID_W)[:, None] * freqs[None, :], (t % GRID_W)[:, None] * freqs[None, :]], axis=1)
    cos = np.zeros((n_lat + n_ctx, LANES))
    sin = np.zeros((n_lat + n_ctx, LANES))
    w = 2 * n_freq
    cos[:n_lat, :w] = np.cos(ang)
    cos[:n_lat, half:half + w] = np.cos(ang)
    sin[:n_lat, :w] = -np.sin(ang)
    sin[:n_lat, half:half + w] = np.sin(ang)
    cos[n_lat:, :w] = 1.0
    cos[n_lat:, half:half + w] = 1.0
    return jnp.asarray(cos, F32), jnp.asarray(sin, F32)


def _rope_perm(n_freq):
    idx = np.full((LANES,), -1, np.int64)
    f = n_freq
    idx[0:f] = np.arange(0, f)
    idx[f:2 * f] = np.arange(2 * f, 3 * f)
    idx[64:64 + f] = np.arange(f, 2 * f)
    idx[64 + f:64 + 2 * f] = np.arange(3 * f, 4 * f)
    return idx


def _take_cols(w, idx):
    wz = jnp.concatenate([w, jnp.zeros((w.shape[0], 1), w.dtype)], axis=1)
    return wz[:, np.where(idx < 0, w.shape[1], idx)]


def _na_bias_tables(rpb, rows):
    kh, kw_ = B_MAX_KH, B_KW
    i = np.arange(NA_Q_ROWS)[:, None]
    j = np.arange(NA_K_ROWS)[None, :]
    cases = []
    for r0 in (0, NA_Q_ROWS, rows - NA_Q_ROWS):
        start = int(np.clip(r0 - kh // 2, 0, rows - NA_K_ROWS))
        rs = np.clip(r0 + i - kh // 2, 0, rows - kh)
        key_row = start + j
        cases.append(((key_row >= rs) & (key_row < rs + kh), key_row - (r0 + i) + (kh - 1)))
    row_ok = np.stack([c[0] for c in cases])
    dr = np.stack([c[1] for c in cases])
    col = np.arange(GRID_W)
    cs = np.clip(col - kw_ // 2, 0, GRID_W - kw_)
    kcol = np.arange(GRID_W)[None, :]
    col_ok = (kcol >= cs[:, None]) & (kcol < cs[:, None] + kw_)
    dc = kcol - col[:, None] + (kw_ - 1)
    ok = row_ok[:, :, None, :, None] & col_ok[None, None, :, None, :]
    dr_i = np.broadcast_to(np.clip(dr, 0, 2 * kh - 2)[:, :, None, :, None], ok.shape)
    dc_i = np.broadcast_to(np.clip(dc, 0, 2 * kw_ - 2)[None, None, :, None, :], ok.shape)
    vals = rpb[:, dr_i, dc_i] * LOG2E
    tab = jnp.where(ok[None], vals, NEG_BIG)
    return tab.reshape(rpb.shape[0], 3, NA_Q_ROWS * GRID_W, NA_K_ROWS * GRID_W)


def kernel(x, c, ctx, c_ctx, w_ada, b_ada, ln_g, ln_b, w_ffn_gate, w_ffn_up, w_ffn_down, a_w_qkv, a_q_gain, a_k_gain, a_w_o, b_w_qkv, b_rpb, b_w_o, c_w_dqkv, c_q_a_gain, c_kv_a_gain, c_w_uq, c_w_ukv, c_w_o):
    bsz, n_lat, d = x.shape
    n_ctx = ctx.shape[1]
    t = n_lat + n_ctx
    depth = w_ada.shape[0]
    rows = n_lat // GRID_W
    assert n_lat % ROW_TILE == 0 and n_ctx == ROW_TILE and bsz < MOD_ROWS
    assert n_lat % (NA_Q_ROWS * GRID_W) == 0 and rows >= NA_K_ROWS
    assert a_w_qkv.shape[2] == (A_HEADS + 2 * A_KV_HEADS) * A_HEAD_DIM and b_rpb.shape[1] == B_HEADS
    assert c_w_dqkv.shape[2] == C_Q_RANK + C_KV_RANK + C_ROPE and d == B_HEADS * B_HEAD_DIM
    alpha = (2.0 * depth) ** 0.25

    xa = jnp.concatenate([x, ctx], axis=1)
    cond = jnp.zeros((MOD_ROWS, d), F32).at[:bsz].set(c).at[bsz].set(c_ctx)
    mod = _ada_call(cond, w_ada, b_ada)

    cos_a, sin_a = _rope_tables(n_lat, n_ctx, A_HEAD_DIM // 4)
    cos_c, sin_c = _rope_tables(n_lat, n_ctx, C_ROPE // 4)
    perm_a = _rope_perm(A_HEAD_DIM // 4)
    perm_c = _rope_perm(C_ROPE // 4)

    lat_chunks = ((0, 512, n_lat // 512), (n_lat, n_ctx, 1))
    ctx_chunks = ((n_lat, n_ctx, 1),)

    for i in range(depth):
        last = i == depth - 1
        out_rows = n_lat if last else t
        kind, j = i % 3, i // 3
        if kind == 0:
            hd = A_HEAD_DIM
            nqk = A_HEADS + A_KV_HEADS
            col_idx = np.concatenate([h * hd + perm_a for h in range(nqk)]
                                     + [np.arange(nqk * hd, (nqk + A_KV_HEADS) * hd)])
            w = a_w_qkv[j][:, col_idx].astype(BF16)
            qkv = _qkv_a_call(xa, mod, i, w, a_q_gain[j][perm_a][None, :], a_k_gain[j][perm_a][None, :],
                              cos_a, sin_a, n_lat)
            rep = A_HEADS // A_KV_HEADS
            common = dict(mode="heads", n_groups=A_KV_HEADS, n_heads=rep, dk=hd, dv=hd, q_col0=0,
                          k_col0=A_HEADS, v_col0=A_HEADS + A_KV_HEADS, out_rows=out_rows)
            o = _flash_call(qkv, qkv, qkv, name="attn_gqa", tq=128, q_tile0=0, n_q_tiles=n_lat // 128,
                            chunks=lat_chunks, **common)
            if not last:
                o = _flash_call(qkv, qkv, qkv, name="attn_gqa_ctx", tq=n_ctx, q_tile0=n_lat // n_ctx, n_q_tiles=1,
                                chunks=ctx_chunks, prev=o, **common)
            wo = a_w_o[j]
        elif kind == 1:
            qkv = _qkv_b_call(xa, mod, i, b_w_qkv[j].astype(BF16), n_lat)
            o = _na_call(qkv, _na_bias_tables(b_rpb[j], rows), n_lat, n_ctx, out_rows)
            if not last:
                n_pairs = d // LANES
                o = _flash_call(qkv, qkv, qkv, name="attn_na_ctx", mode="pair", n_groups=n_pairs, n_heads=2,
                                dk=LANES, dv=LANES, tq=n_ctx, q_tile0=n_lat // n_ctx, n_q_tiles=1, q_col0=0,
                                k_col0=n_pairs, v_col0=2 * n_pairs, chunks=ctx_chunks, out_rows=out_rows, prev=o)
            wo = b_w_o[j]
        else:
            r1 = C_Q_RANK + C_KV_RANK
            wd_idx = np.concatenate([np.arange(r1), np.where(perm_c < 0, -1, r1 + perm_c)])
            wdn = _take_cols(c_w_dqkv[j], wd_idx).astype(BF16)
            hq = C_NOPE + C_ROPE
            uq_idx = np.concatenate([np.concatenate([h * hq + np.arange(C_NOPE),
                                                     np.where(perm_c < 0, -1, h * hq + C_NOPE + perm_c)])
                                     for h in range(C_HEADS)])
            wuq = _take_cols(c_w_uq[j], uq_idx).astype(BF16)
            qc, kc, vc = _proj_c_call(xa, mod, i, wdn, c_q_a_gain[j][None, :], c_kv_a_gain[j][None, :], wuq,
                                      c_w_ukv[j].astype(BF16), cos_c, sin_c, n_lat)
            common = dict(mode="heads", n_groups=C_HEADS, n_heads=1, dk=2 * LANES, dv=C_V, q_col0=0, k_col0=0,
                          v_col0=0, out_rows=out_rows)
            o = _flash_call(qc, kc, vc, name="attn_mla", tq=512, q_tile0=0, n_q_tiles=n_lat // 512,
                            chunks=lat_chunks, **common)
            if not last:
                o = _flash_call(qc, kc, vc, name="attn_mla_ctx", tq=n_ctx, q_tile0=n_lat // n_ctx, n_q_tiles=1,
                                chunks=ctx_chunks, prev=o, **common)
            wo = c_w_o[j]
        xa = _post_call(o, xa, mod, i, wo.astype(BF16), w_ffn_gate[i].astype(BF16), w_ffn_up[i].astype(BF16),
                        w_ffn_down[i].astype(BF16), ln_g, ln_b, n_lat, out_rows, alpha)
    return xa
```

```python
import functools
import math

import numpy as np
import jax
import jax.numpy as jnp
from jax import lax
from jax.experimental import pallas as pl
from jax.experimental.pallas import tpu as pltpu

GRID_W = 64
A_HEADS, A_KV_HEADS, A_HEAD_DIM = 8, 2, 128
B_HEADS, B_HEAD_DIM, B_MAX_KH, B_KW = 16, 64, 8, 16
C_HEADS, C_NOPE, C_ROPE, C_V, C_Q_RANK, C_KV_RANK = 8, 128, 64, 128, 384, 256
ROPE_BASE = 10000.0
RMS_EPS = 1e-6
LN_EPS = 1e-5
LOG2E = math.log2(math.e)
NEG_BIG = -1e30

LANES = 128
V7X_VMEM_BYTES = 64 * 1024 * 1024
VMEM_LIMIT = 56 * 1024 * 1024

ROW_TILE = 256
MOD_ROWS = 8
NA_Q_ROWS = 4
NA_K_ROWS = 12
NA_PAIRS_PER_STEP = 4

BF16 = jnp.bfloat16
F32 = jnp.float32


def _cparams(n_axes, vmem=None):
    return pltpu.CompilerParams(dimension_semantics=("arbitrary",) * n_axes, vmem_limit_bytes=vmem)


def _resident(shape, index_map):
    return pl.BlockSpec(shape, index_map, pipeline_mode=pl.Buffered(1))


def _ada_kernel(c_ref, w_ref, b_ref, o_ref):
    c = c_ref[...]
    cs = c / (1.0 + jnp.exp(-c))
    o_ref[...] = jnp.dot(cs, w_ref[...], preferred_element_type=F32) + b_ref[...]


def _ada_call(cond, w_ada, b_ada):
    depth, d, n6 = w_ada.shape
    tn = n6 // 4
    return pl.pallas_call(
        _ada_kernel,
        grid=(depth, n6 // tn),
        in_specs=[
            pl.BlockSpec((MOD_ROWS, d), lambda l, j: (0, 0)),
            pl.BlockSpec((None, d, tn), lambda l, j: (l, 0, j)),
            pl.BlockSpec((None, 1, tn), lambda l, j: (l, 0, j)),
        ],
        out_specs=pl.BlockSpec((None, MOD_ROWS, tn), lambda l, j: (l, 0, j)),
        out_shape=jax.ShapeDtypeStruct((depth, MOD_ROWS, n6), F32),
        compiler_params=_cparams(2, 40 * 1024 * 1024),
        name="ada_mod",
    )(cond, w_ada, b_ada.reshape(depth, 1, n6))


def _mod_row(n_lat_tiles, ctx_row):
    b, i = pl.program_id(0), pl.program_id(1)
    return jnp.where(i < n_lat_tiles, b, ctx_row)


def _mod_vec(mod_ref, row, k, d):
    return mod_ref[pl.ds(row, 1), k * d:(k + 1) * d]


def _modulated(x_ref, mod_ref, row, k_shift, d):
    sh = _mod_vec(mod_ref, row, k_shift, d)
    sc = _mod_vec(mod_ref, row, k_shift + 1, d)
    return (x_ref[...] * (1.0 + sc) + sh).astype(BF16)


def _rms(t, g):
    return t * lax.rsqrt(jnp.mean(t * t, axis=-1, keepdims=True) + RMS_EPS) * g


def _rope(t, cos, sin):
    return t * cos + pltpu.roll(t, LANES // 2, 1) * sin


def _layer_norm(t, g, b):
    mu = jnp.mean(t, axis=-1, keepdims=True)
    c = t - mu
    var = jnp.mean(c * c, axis=-1, keepdims=True)
    return c * lax.rsqrt(var + LN_EPS) * g + b


def _qkv_a_kernel(x_ref, mod_ref, w_ref, gq_ref, gk_ref, cos_ref, sin_ref, o_ref, *, n_lat_tiles, ctx_row, q_scale):
    d = x_ref.shape[-1]
    row = _mod_row(n_lat_tiles, ctx_row)
    h = _modulated(x_ref, mod_ref, row, 0, d)
    acc = jnp.dot(h, w_ref[...], preferred_element_type=F32)
    cos, sin = cos_ref[...], sin_ref[...]
    gq = gq_ref[...] * q_scale
    gk = gk_ref[...]
    hd = A_HEAD_DIM
    for j in range(A_HEADS + A_KV_HEADS):
        t = acc[:, j * hd:(j + 1) * hd]
        y = _rope(_rms(t, gq if j < A_HEADS else gk), cos, sin)
        o_ref[:, j * hd:(j + 1) * hd] = y.astype(BF16)
    v0 = (A_HEADS + A_KV_HEADS) * hd
    o_ref[:, v0:] = acc[:, v0:].astype(BF16)


def _qkv_a_call(xa, mod, layer, w, gq, gk, cos, sin, n_lat):
    bsz, t, d = xa.shape
    n_out = w.shape[1]
    tm = ROW_TILE
    kern = functools.partial(_qkv_a_kernel, n_lat_tiles=n_lat // tm, ctx_row=bsz,
                             q_scale=A_HEAD_DIM ** -0.5 * LOG2E)
    return pl.pallas_call(
        kern,
        grid=(bsz, t // tm),
        in_specs=[
            pl.BlockSpec((None, tm, d), lambda b, i: (b, i, 0)),
            pl.BlockSpec((None, MOD_ROWS, 6 * d), lambda b, i: (layer, 0, 0)),
            _resident((d, n_out), lambda b, i: (0, 0)),
            pl.BlockSpec((1, A_HEAD_DIM), lambda b, i: (0, 0)),
            pl.BlockSpec((1, A_HEAD_DIM), lambda b, i: (0, 0)),
            pl.BlockSpec((tm, LANES), lambda b, i: (i, 0)),
            pl.BlockSpec((tm, LANES), lambda b, i: (i, 0)),
        ],
        out_specs=pl.BlockSpec((None, tm, n_out), lambda b, i: (b, i, 0)),
        out_shape=jax.ShapeDtypeStruct((bsz, t, n_out), BF16),
        compiler_params=_cparams(2, 32 * 1024 * 1024),
        name="qkv_gqa",
    )(xa, mod, w, gq, gk, cos, sin)


def _qkv_b_kernel(x_ref, mod_ref, w_ref, o_ref, *, n_lat_tiles, ctx_row, q_scale):
    d = x_ref.shape[-1]
    row = _mod_row(n_lat_tiles, ctx_row)
    h = _modulated(x_ref, mod_ref, row, 0, d)
    acc = jnp.dot(h, w_ref[...], preferred_element_type=F32)
    o_ref[:, :d] = (acc[:, :d] * q_scale).astype(BF16)
    o_ref[:, d:] = acc[:, d:].astype(BF16)


def _qkv_b_call(xa, mod, layer, w, n_lat):
    bsz, t, d = xa.shape
    n_out = w.shape[1]
    tm = ROW_TILE
    kern = functools.partial(_qkv_b_kernel, n_lat_tiles=n_lat // tm, ctx_row=bsz,
                             q_scale=B_HEAD_DIM ** -0.5 * LOG2E)
    return pl.pallas_call(
        kern,
        grid=(bsz, t // tm),
        in_specs=[
            pl.BlockSpec((None, tm, d), lambda b, i: (b, i, 0)),
            pl.BlockSpec((None, MOD_ROWS, 6 * d), lambda b, i: (layer, 0, 0)),
            _resident((d, n_out), lambda b, i: (0, 0)),
        ],
        out_specs=pl.BlockSpec((None, tm, n_out), lambda b, i: (b, i, 0)),
        out_shape=jax.ShapeDtypeStruct((bsz, t, n_out), BF16),
        compiler_params=_cparams(2, 40 * 1024 * 1024),
        name="qkv_na",
    )(xa, mod, w)


def _proj_c_kernel(x_ref, mod_ref, wd_ref, gq_ref, gkv_ref, wuq_ref, wukv_ref, cos_ref, sin_ref,
                   q_ref, k_ref, v_ref, *, n_lat_tiles, ctx_row, q_scale):
    d = x_ref.shape[-1]
    row = _mod_row(n_lat_tiles, ctx_row)
    h = _modulated(x_ref, mod_ref, row, 0, d)
    cos, sin = cos_ref[...], sin_ref[...]
    dn = jnp.dot(h, wd_ref[...], preferred_element_type=F32)
    r0, r1 = C_Q_RANK, C_Q_RANK + C_KV_RANK
    ql = _rms(dn[:, :r0], gq_ref[...]).astype(BF16)
    kvl = _rms(dn[:, r0:r1], gkv_ref[...]).astype(BF16)
    kr = _rope(dn[:, r1:], cos, sin).astype(BF16)
    q = jnp.dot(ql, wuq_ref[...], preferred_element_type=F32)
    kv = jnp.dot(kvl, wukv_ref[...], preferred_element_type=F32)
    w2 = 2 * LANES
    for hh in range(C_HEADS):
        q_ref[:, hh * w2:hh * w2 + LANES] = (q[:, hh * w2:hh * w2 + LANES] * q_scale).astype(BF16)
        qr = _rope(q[:, hh * w2 + LANES:(hh + 1) * w2], cos, sin) * q_scale
        q_ref[:, hh * w2 + LANES:(hh + 1) * w2] = qr.astype(BF16)
        k_ref[:, hh * w2:hh * w2 + LANES] = kv[:, hh * w2:hh * w2 + LANES].astype(BF16)
        k_ref[:, hh * w2 + LANES:(hh + 1) * w2] = kr
        v_ref[:, hh * LANES:(hh + 1) * LANES] = kv[:, hh * w2 + LANES:(hh + 1) * w2].astype(BF16)


def _proj_c_call(xa, mod, layer, wd, gq, gkv, wuq, wukv, cos, sin, n_lat):
    bsz, t, d = xa.shape
    tm = ROW_TILE
    kern = functools.partial(_proj_c_kernel, n_lat_tiles=n_lat // tm, ctx_row=bsz,
                             q_scale=(C_NOPE + C_ROPE) ** -0.5 * LOG2E)
    const = lambda b, i: (0, 0)
    qk_w = C_HEADS * 2 * LANES
    return pl.pallas_call(
        kern,
        grid=(bsz, t // tm),
        in_specs=[
            pl.BlockSpec((None, tm, d), lambda b, i: (b, i, 0)),
            pl.BlockSpec((None, MOD_ROWS, 6 * d), lambda b, i: (layer, 0, 0)),
            _resident(wd.shape, const),
            pl.BlockSpec(gq.shape, const),
            pl.BlockSpec(gkv.shape, const),
            _resident(wuq.shape, const),
            _resident(wukv.shape, const),
            pl.BlockSpec((tm, LANES), lambda b, i: (i, 0)),
            pl.BlockSpec((tm, LANES), lambda b, i: (i, 0)),
        ],
        out_specs=[
            pl.BlockSpec((None, tm, qk_w), lambda b, i: (b, i, 0)),
            pl.BlockSpec((None, tm, qk_w), lambda b, i: (b, i, 0)),
            pl.BlockSpec((None, tm, C_HEADS * C_V), lambda b, i: (b, i, 0)),
        ],
        out_shape=[
            jax.ShapeDtypeStruct((bsz, t, qk_w), BF16),
            jax.ShapeDtypeStruct((bsz, t, qk_w), BF16),
            jax.ShapeDtypeStruct((bsz, t, C_HEADS * C_V), BF16),
        ],
        compiler_params=_cparams(2, 40 * 1024 * 1024),
        name="proj_mla",
    )(xa, mod, wd, gq, gkv, wuq, wukv, cos, sin)


def _pair_masks(shape):
    lane = lax.broadcasted_iota(jnp.int32, shape, 1)
    return lane < (LANES // 2)


def _stack_queries(q_ref, mode, n_heads, dk):
    if mode == "pair":
        qb = q_ref[...]
        lo = _pair_masks(qb.shape)
        zero = jnp.zeros_like(qb)
        return jnp.concatenate([jnp.where(lo, qb, zero), jnp.where(lo, zero, qb)], axis=0)
    if n_heads == 1:
        return q_ref[...]
    return jnp.concatenate([q_ref[:, r * dk:(r + 1) * dk] for r in range(n_heads)], axis=0)


def _unstack_outputs(o, o_ref, mode, n_heads, dv, tq):
    if mode == "pair":
        lo = _pair_masks((tq, LANES))
        o_ref[...] = jnp.where(lo, o[:tq], o[tq:]).astype(o_ref.dtype)
    else:
        for r in range(n_heads):
            o_ref[:, r * dv:(r + 1) * dv] = o[r * tq:(r + 1) * tq].astype(o_ref.dtype)


def _dot_nt(a, b):
    return lax.dot_general(a, b, (((1,), (1,)), ((), ())), preferred_element_type=F32)


def _flash_kernel(q_ref, k_ref, v_ref, *rest, mode, n_heads, dk, dv, chunks):
    o_ref = rest[-1]
    tq = q_ref.shape[0]
    q = _stack_queries(q_ref, mode, n_heads, dk)
    m_rows = q.shape[0]

    def step(start, size, carry):
        m, l, acc = carry
        k = k_ref[pl.ds(start, size), :]
        v = v_ref[pl.ds(start, size), :]
        s = _dot_nt(q, k)
        m_new = jnp.maximum(m, jnp.max(s, axis=-1, keepdims=True))
        alpha = jnp.exp2(m - m_new)
        p = jnp.exp2(s - m_new)
        l = alpha * l + jnp.sum(p, axis=-1, keepdims=True)
        acc = alpha * acc + jnp.dot(p.astype(BF16), v, preferred_element_type=F32)
        return m_new, l, acc

    carry = (jnp.full((m_rows, 1), NEG_BIG, F32), jnp.zeros((m_rows, 1), F32), jnp.zeros((m_rows, dv), F32))
    for start, size, count in chunks:
        for c in range(count):
            carry = step(start + c * size, size, carry)
    _, l, acc = carry
    _unstack_outputs(acc / l, o_ref, mode, n_heads, dv, tq)


def _flash_call(q_arr, k_arr, v_arr, *, name, mode, n_groups, n_heads, dk, dv, tq, q_tile0, n_q_tiles,
                q_col0, k_col0, v_col0, chunks, out_rows, prev=None):
    bsz, t, _ = k_arr.shape
    q_w = LANES if mode == "pair" else n_heads * dk
    o_w = LANES if mode == "pair" else n_heads * dv
    kern = functools.partial(_flash_kernel, mode=mode, n_heads=n_heads, dk=dk, dv=dv, chunks=chunks)
    in_specs = [
        pl.BlockSpec((None, tq, q_w), lambda b, g, i: (b, i + q_tile0, q_col0 + g)),
        pl.BlockSpec((None, t, dk), lambda b, g, i: (b, 0, k_col0 + g)),
        pl.BlockSpec((None, t, dv), lambda b, g, i: (b, 0, v_col0 + g)),
    ]
    args = [q_arr, k_arr, v_arr]
    aliases = {}
    if prev is not None:
        in_specs.append(pl.BlockSpec(memory_space=pl.ANY))
        args.append(prev)
        aliases = {3: 0}
    return pl.pallas_call(
        kern,
        grid=(bsz, n_groups, n_q_tiles),
        in_specs=in_specs,
        out_specs=pl.BlockSpec((None, tq, o_w), lambda b, g, i: (b, i + q_tile0, g)),
        out_shape=jax.ShapeDtypeStruct((bsz, out_rows, n_groups * o_w), BF16),
        input_output_aliases=aliases,
        compiler_params=_cparams(3, 40 * 1024 * 1024),
        name=name,
    )(*args)


def _na_kernel(q_ref, k_ref, v_ref, bias_ref, o_ref, *, n_lat, n_ctx, n_groups):
    gi = pl.program_id(2)
    rows = n_groups * NA_Q_ROWS
    tq = NA_Q_ROWS * GRID_W
    tk = NA_K_ROWS * GRID_W
    start = jnp.clip(gi * NA_Q_ROWS - B_MAX_KH // 2, 0, rows - NA_K_ROWS) * GRID_W
    start = pl.multiple_of(start, GRID_W)
    lo = _pair_masks((tq, LANES))
    for s in range(NA_PAIRS_PER_STEP):
        cols = slice(s * LANES, (s + 1) * LANES)
        qb = q_ref[:, cols]
        zero = jnp.zeros_like(qb)
        q = jnp.concatenate([jnp.where(lo, qb, zero), jnp.where(lo, zero, qb)], axis=0)
        kw = k_ref[pl.ds(start, tk), cols]
        vw = v_ref[pl.ds(start, tk), cols]
        kc = k_ref[n_lat:n_lat + n_ctx, cols]
        vc = v_ref[n_lat:n_lat + n_ctx, cols]
        bias = bias_ref[2 * s:2 * s + 2].reshape(2 * tq, tk)
        s_w = _dot_nt(q, kw) + bias
        s_c = _dot_nt(q, kc)
        m = jnp.maximum(jnp.max(s_w, axis=-1, keepdims=True), jnp.max(s_c, axis=-1, keepdims=True))
        p_w = jnp.exp2(s_w - m)
        p_c = jnp.exp2(s_c - m)
        l = jnp.sum(p_w, axis=-1, keepdims=True) + jnp.sum(p_c, axis=-1, keepdims=True)
        o = (jnp.dot(p_w.astype(BF16), vw, preferred_element_type=F32)
             + jnp.dot(p_c.astype(BF16), vc, preferred_element_type=F32))
        o = o / l
        o_ref[:, cols] = jnp.where(lo, o[:tq], o[tq:]).astype(o_ref.dtype)


def _na_call(qkv, bias, n_lat, n_ctx, out_rows):
    bsz, t, w3 = qkv.shape
    d = w3 // 3
    w = NA_PAIRS_PER_STEP * LANES
    n_steps = d // w
    tq = NA_Q_ROWS * GRID_W
    tk = NA_K_ROWS * GRID_W
    n_groups = n_lat // tq
    kern = functools.partial(_na_kernel, n_lat=n_lat, n_ctx=n_ctx, n_groups=n_groups)

    def bias_map(b, p, g):
        case = jnp.where(g == 0, 0, jnp.where(g == n_groups - 1, 2, 1))
        return (p, case, 0, 0)

    return pl.pallas_call(
        kern,
        grid=(bsz, n_steps, n_groups),
        in_specs=[
            pl.BlockSpec((None, tq, w), lambda b, p, g: (b, g, p)),
            pl.BlockSpec((None, t, w), lambda b, p, g: (b, 0, n_steps + p)),
            pl.BlockSpec((None, t, w), lambda b, p, g: (b, 0, 2 * n_steps + p)),
            pl.BlockSpec((2 * NA_PAIRS_PER_STEP, None, tq, tk), bias_map),
        ],
        out_specs=pl.BlockSpec((None, tq, w), lambda b, p, g: (b, g, p)),
        out_shape=jax.ShapeDtypeStruct((bsz, out_rows, d), BF16),
        compiler_params=_cparams(3, 48 * 1024 * 1024),
        name="attn_na",
    )(qkv, qkv, qkv, bias)


def _post_kernel(o_ref, x_ref, mod_ref, wo_ref, wg_ref, wu_ref, wd_ref, lng_ref, lnb_ref, out_ref,
                 *, n_lat_tiles, ctx_row, alpha, ff_chunk):
    d = x_ref.shape[-1]
    row = _mod_row(n_lat_tiles, ctx_row)
    g1 = _mod_vec(mod_ref, row, 2, d)
    sh2 = _mod_vec(mod_ref, row, 3, d)
    sc2 = _mod_vec(mod_ref, row, 4, d)
    g2 = _mod_vec(mod_ref, row, 5, d)
    y = jnp.dot(o_ref[...], wo_ref[...], preferred_element_type=F32)
    x1 = _layer_norm(alpha * x_ref[...] + g1 * y, lng_ref[0:1, :], lnb_ref[0:1, :])
    h = (x1 * (1.0 + sc2) + sh2).astype(BF16)
    d_ff = wg_ref.shape[1]
    f = jnp.zeros(x1.shape, F32)
    for c0 in range(0, d_ff, ff_chunk):
        gate = jnp.dot(h, wg_ref[:, c0:c0 + ff_chunk], preferred_element_type=F32)
        up = jnp.dot(h, wu_ref[:, c0:c0 + ff_chunk], preferred_element_type=F32)
        a = (gate / (1.0 + jnp.exp(-gate)) * up).astype(BF16)
        f = f + jnp.dot(a, wd_ref[c0:c0 + ff_chunk, :], preferred_element_type=F32)
    out_ref[...] = _layer_norm(alpha * x1 + g2 * f, lng_ref[1:2, :], lnb_ref[1:2, :])


def _post_call(o, xa, mod, layer, wo, wg, wu, wd, lng, lnb, n_lat, out_rows, alpha):
    bsz, _, d = xa.shape
    tm = ROW_TILE
    d_ff = wg.shape[1]
    kern = functools.partial(_post_kernel, n_lat_tiles=n_lat // tm, ctx_row=bsz, alpha=alpha, ff_chunk=256)
    const = lambda b, i: (0, 0)
    return pl.pallas_call(
        kern,
        grid=(bsz, out_rows // tm),
        in_specs=[
            pl.BlockSpec((None, tm, o.shape[-1]), lambda b, i: (b, i, 0)),
            pl.BlockSpec((None, tm, d), lambda b, i: (b, i, 0)),
            pl.BlockSpec((None, MOD_ROWS, 6 * d), lambda b, i: (layer, 0, 0)),
            _resident(wo.shape, const),
            _resident((d, d_ff), const),
            _resident((d, d_ff), const),
            _resident((d_ff, d), const),
            pl.BlockSpec((None, 2, d), lambda b, i: (layer, 0, 0)),
            pl.BlockSpec((None, 2, d), lambda b, i: (layer, 0, 0)),
        ],
        out_specs=pl.BlockSpec((None, tm, d), lambda b, i: (b, i, 0)),
        out_shape=jax.ShapeDtypeStruct((bsz, out_rows, d), F32),
        compiler_params=_cparams(2, VMEM_LIMIT),
        name="post_ffn",
    )(o, xa, mod, wo, wg, wu, wd, lng, lnb)


def _rope_tables(n_lat, n_ctx, n_freq):
    half = LANES // 2
    t = np.arange(n_lat)
    freqs = ROPE_BASE ** (-np.arange(0, 2 * n_freq, 2, dtype=np.float64) / (2 * n_freq))
    ang = np.concatenate([(t // GRID_W)[:, None] * freqs[None, :], (t % GRID_W)[:, None] * freqs[None, :]], axis=1)
    cos = np.zeros((n_lat + n_ctx, LANES))
    sin = np.zeros((n_lat + n_ctx, LANES))
    w = 2 * n_freq
    cos[:n_lat, :w] = np.cos(ang)
    cos[:n_lat, half:half + w] = np.cos(ang)
    sin[:n_lat, :w] = -np.sin(ang)
    sin[:n_lat, half:half + w] = np.sin(ang)
    cos[n_lat:, :w] = 1.0
    cos[n_lat:, half:half + w] = 1.0
    return jnp.asarray(cos, F32), jnp.asarray(sin, F32)


def _rope_perm(n_freq):
    idx = np.full((LANES,), -1, np.int64)
    f = n_freq
    idx[0:f] = np.arange(0, f)
    idx[f:2 * f] = np.arange(2 * f, 3 * f)
    idx[64:64 + f] = np.arange(f, 2 * f)
    idx[64 + f:64 + 2 * f] = np.arange(3 * f, 4 * f)
    return idx


def _take_cols(w, idx):
    wz = jnp.concatenate([w, jnp.zeros((w.shape[0], 1), w.dtype)], axis=1)
    return wz[:, np.where(idx < 0, w.shape[1], idx)]


def _na_row_cases(rows):
    kh = B_MAX_KH
    i = np.arange(NA_Q_ROWS)[:, None]
    j = np.arange(NA_K_ROWS)[None, :]
    cases = []
    for r0 in (0, NA_Q_ROWS, rows - NA_Q_ROWS):
        start = int(np.clip(r0 - kh // 2, 0, rows - NA_K_ROWS))
        rs = np.clip(r0 + i - kh // 2, 0, rows - kh)
        key_row = start + j
        cases.append(((key_row >= rs) & (key_row < rs + kh), key_row - (r0 + i) + (kh - 1)))
    return np.stack([c[0] for c in cases]), np.stack([c[1] for c in cases])


def _na_bias_kernel(r2_ref, o_ref, *, row_ok, dr_idx):
    n_e = r2_ref.shape[0]
    shape = (GRID_W, LANES)
    c = lax.broadcasted_iota(jnp.int32, shape, 0)
    lane = lax.broadcasted_iota(jnp.int32, shape, 1)
    kc = jnp.where(lane < GRID_W, lane, lane - GRID_W)
    cs = jnp.clip(c - B_KW // 2, 0, GRID_W - B_KW)
    col_ok = (kc >= cs) & (kc < cs + B_KW)
    left = lane < GRID_W
    neg = jnp.full(shape, NEG_BIG, F32)
    x = r2_ref[...]
    tiles = []
    for e in range(n_e):
        t = pltpu.roll(jnp.broadcast_to(x[e:e + 1, :], shape), LANES - (B_KW - 1), 1, stride=1, stride_axis=0)
        tiles.append(jnp.where(col_ok, t * LOG2E, neg))
    for case in range(row_ok.shape[0]):
        for i in range(NA_Q_ROWS):
            for jp in range(NA_K_ROWS // 2):
                ok_l, ok_r = bool(row_ok[case, i, 2 * jp]), bool(row_ok[case, i, 2 * jp + 1])
                e = int(np.clip(dr_idx[case, i, 2 * jp] + 1, 0, n_e - 1))
                if ok_l and ok_r:
                    tile = tiles[e]
                elif ok_l:
                    tile = jnp.where(left, tiles[e], neg)
                elif ok_r:
                    tile = jnp.where(left, neg, tiles[e])
                else:
                    tile = neg
                o_ref[case, i * GRID_W:(i + 1) * GRID_W, jp * LANES:(jp + 1) * LANES] = tile


def _na_bias_tables(rpb, rows):
    n_heads, n_dr, n_dc = rpb.shape
    n_e = n_dr + 1
    r2 = jnp.zeros((n_heads, n_e, LANES), F32)
    r2 = r2.at[:, 1:, :n_dc].set(rpb).at[:, :n_dr, GRID_W:GRID_W + n_dc].set(rpb)
    row_ok, dr_idx = _na_row_cases(rows)
    kern = functools.partial(_na_bias_kernel, row_ok=row_ok, dr_idx=dr_idx)
    tq, tk = NA_Q_ROWS * GRID_W, NA_K_ROWS * GRID_W
    return pl.pallas_call(
        kern,
        grid=(n_heads,),
        in_specs=[pl.BlockSpec((None, n_e, LANES), lambda h: (h, 0, 0))],
        out_specs=pl.BlockSpec((None, 3, tq, tk), lambda h: (h, 0, 0, 0)),
        out_shape=jax.ShapeDtypeStruct((n_heads, 3, tq, tk), F32),
        compiler_params=_cparams(1, 32 * 1024 * 1024),
        name="na_bias",
    )(r2)


def kernel(x, c, ctx, c_ctx, w_ada, b_ada, ln_g, ln_b, w_ffn_gate, w_ffn_up, w_ffn_down, a_w_qkv, a_q_gain, a_k_gain, a_w_o, b_w_qkv, b_rpb, b_w_o, c_w_dqkv, c_q_a_gain, c_kv_a_gain, c_w_uq, c_w_ukv, c_w_o):
    bsz, n_lat, d = x.shape
    n_ctx = ctx.shape[1]
    t = n_lat + n_ctx
    depth = w_ada.shape[0]
    rows = n_lat // GRID_W
    assert n_lat % ROW_TILE == 0 and n_ctx == ROW_TILE and bsz < MOD_ROWS
    assert n_lat % (NA_Q_ROWS * GRID_W) == 0 and rows >= NA_K_ROWS
    assert a_w_qkv.shape[2] == (A_HEADS + 2 * A_KV_HEADS) * A_HEAD_DIM and b_rpb.shape[1] == B_HEADS
    assert c_w_dqkv.shape[2] == C_Q_RANK + C_KV_RANK + C_ROPE and d == B_HEADS * B_HEAD_DIM
    alpha = (2.0 * depth) ** 0.25

    xa = jnp.concatenate([x, ctx], axis=1)
    cond = jnp.zeros((MOD_ROWS, d), F32).at[:bsz].set(c).at[bsz].set(c_ctx)
    mod = _ada_call(cond, w_ada, b_ada)

    cos_a, sin_a = _rope_tables(n_lat, n_ctx, A_HEAD_DIM // 4)
    cos_c, sin_c = _rope_tables(n_lat, n_ctx, C_ROPE // 4)
    perm_a = _rope_perm(A_HEAD_DIM // 4)
    perm_c = _rope_perm(C_ROPE // 4)

    lat_chunks = ((0, 512, n_lat // 512), (n_lat, n_ctx, 1))
    ctx_chunks = ((n_lat, n_ctx, 1),)

    for i in range(depth):
        last = i == depth - 1
        out_rows = n_lat if last else t
        kind, j = i % 3, i // 3
        if kind == 0:
            hd = A_HEAD_DIM
            nqk = A_HEADS + A_KV_HEADS
            col_idx = np.concatenate([h * hd + perm_a for h in range(nqk)]
                                     + [np.arange(nqk * hd, (nqk + A_KV_HEADS) * hd)])
            w = a_w_qkv[j][:, col_idx].astype(BF16)
            qkv = _qkv_a_call(xa, mod, i, w, a_q_gain[j][perm_a][None, :], a_k_gain[j][perm_a][None, :],
                              cos_a, sin_a, n_lat)
            rep = A_HEADS // A_KV_HEADS
            common = dict(mode="heads", n_groups=A_KV_HEADS, n_heads=rep, dk=hd, dv=hd, q_col0=0,
                          k_col0=A_HEADS, v_col0=A_HEADS + A_KV_HEADS, out_rows=out_rows)
            o = _flash_call(qkv, qkv, qkv, name="attn_gqa", tq=128, q_tile0=0, n_q_tiles=n_lat // 128,
                            chunks=lat_chunks, **common)
            if not last:
                o = _flash_call(qkv, qkv, qkv, name="attn_gqa_ctx", tq=n_ctx, q_tile0=n_lat // n_ctx, n_q_tiles=1,
                                chunks=ctx_chunks, prev=o, **common)
            wo = a_w_o[j]
        elif kind == 1:
            qkv = _qkv_b_call(xa, mod, i, b_w_qkv[j].astype(BF16), n_lat)
            o = _na_call(qkv, _na_bias_tables(b_rpb[j], rows), n_lat, n_ctx, out_rows)
            if not last:
                n_pairs = d // LANES
                o = _flash_call(qkv, qkv, qkv, name="attn_na_ctx", mode="pair", n_groups=n_pairs, n_heads=2,
                                dk=LANES, dv=LANES, tq=n_ctx, q_tile0=n_lat // n_ctx, n_q_tiles=1, q_col0=0,
                                k_col0=n_pairs, v_col0=2 * n_pairs, chunks=ctx_chunks, out_rows=out_rows, prev=o)
            wo = b_w_o[j]
        else:
            r1 = C_Q_RANK + C_KV_RANK
            wd_idx = np.concatenate([np.arange(r1), np.where(perm_c < 0, -1, r1 + perm_c)])
            wdn = _take_cols(c_w_dqkv[j], wd_idx).astype(BF16)
            hq = C_NOPE + C_ROPE
            uq_idx = np.concatenate([np.concatenate([h * hq + np.arange(C_NOPE),
                                                     np.where(perm_c < 0, -1, h * hq + C_NOPE + perm_c)])
                                     for h in range(C_HEADS)])
            wuq = _take_cols(c_w_uq[j], uq_idx).astype(BF16)
            qc, kc, vc = _proj_c_call(xa, mod, i, wdn, c_q_a_gain[j][None, :], c_kv_a_gain[j][None, :], wuq,
                                      c_w_ukv[j].astype(BF16), cos_c, sin_c, n_lat)
            common = dict(mode="heads", n_groups=C_HEADS, n_heads=1, dk=2 * LANES, dv=C_V, q_col0=0, k_col0=0,
                          v_col0=0, out_rows=out_rows)
            o = _flash_call(qc, kc, vc, name="attn_mla", tq=512, q_tile0=0, n_q_tiles=n_lat // 512,
                            chunks=lat_chunks, **common)
            if not last:
                o = _flash_call(qc, kc, vc, name="attn_mla_ctx", tq=n_ctx, q_tile0=n_lat // n_ctx, n_q_tiles=1,
                                chunks=ctx_chunks, prev=o, **common)
            wo = c_w_o[j]
        xa = _post_call(o, xa, mod, i, wo.astype(BF16), w_ffn_gate[i].astype(BF16), w_ffn_up[i].astype(BF16),
                        w_ffn_down[i].astype(BF16), ln_g, ln_b, n_lat, out_rows, alpha)
    return xa
```

```python
import functools
import math

import numpy as np
import jax
import jax.numpy as jnp
from jax import lax
from jax.experimental import pallas as pl
from jax.experimental.pallas import tpu as pltpu

GRID_W = 64
A_HEADS, A_KV_HEADS, A_HEAD_DIM = 8, 2, 128
B_HEADS, B_HEAD_DIM, B_MAX_KH, B_KW = 16, 64, 8, 16
C_HEADS, C_NOPE, C_ROPE, C_V, C_Q_RANK, C_KV_RANK = 8, 128, 64, 128, 384, 256
ROPE_BASE = 10000.0
RMS_EPS = 1e-6
LN_EPS = 1e-5
LOG2E = math.log2(math.e)
NEG_BIG = -1e30

LANES = 128
V7X_VMEM_BYTES = 64 * 1024 * 1024
VMEM_LIMIT = 56 * 1024 * 1024

ROW_TILE = 256
MOD_ROWS = 8
NA_Q_ROWS = 4
NA_K_ROWS = 12
NA_PAIRS_PER_STEP = 4

BF16 = jnp.bfloat16
F32 = jnp.float32


def _cparams(n_axes, vmem=None):
    return pltpu.CompilerParams(dimension_semantics=("arbitrary",) * n_axes, vmem_limit_bytes=vmem)


def _resident(shape, index_map):
    return pl.BlockSpec(shape, index_map, pipeline_mode=pl.Buffered(1))


def _ada_kernel(c_ref, w_ref, b_ref, o_ref):
    c = c_ref[...]
    cs = c / (1.0 + jnp.exp(-c))
    o_ref[...] = jnp.dot(cs, w_ref[...], preferred_element_type=F32) + b_ref[...]


def _ada_call(cond, w_ada, b_ada):
    depth, d, n6 = w_ada.shape
    tn = n6 // 4
    return pl.pallas_call(
        _ada_kernel,
        grid=(depth, n6 // tn),
        in_specs=[
            pl.BlockSpec((MOD_ROWS, d), lambda l, j: (0, 0)),
            pl.BlockSpec((None, d, tn), lambda l, j: (l, 0, j)),
            pl.BlockSpec((None, 1, tn), lambda l, j: (l, 0, j)),
        ],
        out_specs=pl.BlockSpec((None, MOD_ROWS, tn), lambda l, j: (l, 0, j)),
        out_shape=jax.ShapeDtypeStruct((depth, MOD_ROWS, n6), F32),
        compiler_params=_cparams(2, 40 * 1024 * 1024),
        name="ada_mod",
    )(cond, w_ada, b_ada.reshape(depth, 1, n6))


def _mod_row(n_lat_tiles, ctx_row):
    b, i = pl.program_id(0), pl.program_id(1)
    return jnp.where(i < n_lat_tiles, b, ctx_row)


def _mod_vec(mod_ref, row, k, d):
    return mod_ref[pl.ds(row, 1), k * d:(k + 1) * d]


def _modulated(x_ref, mod_ref, row, k_shift, d):
    sh = _mod_vec(mod_ref, row, k_shift, d)
    sc = _mod_vec(mod_ref, row, k_shift + 1, d)
    return (x_ref[...] * (1.0 + sc) + sh).astype(BF16)


def _rms(t, g):
    return t * lax.rsqrt(jnp.mean(t * t, axis=-1, keepdims=True) + RMS_EPS) * g


def _rope(t, cos, sin):
    return t * cos + pltpu.roll(t, LANES // 2, 1) * sin


def _layer_norm(t, g, b):
    mu = jnp.mean(t, axis=-1, keepdims=True)
    c = t - mu
    var = jnp.mean(c * c, axis=-1, keepdims=True)
    return c * lax.rsqrt(var + LN_EPS) * g + b


def _qkv_a_kernel(x_ref, mod_ref, w_ref, gq_ref, gk_ref, cos_ref, sin_ref, o_ref, *, n_lat_tiles, ctx_row, q_scale):
    d = x_ref.shape[-1]
    row = _mod_row(n_lat_tiles, ctx_row)
    h = _modulated(x_ref, mod_ref, row, 0, d)
    acc = jnp.dot(h, w_ref[...], preferred_element_type=F32)
    cos, sin = cos_ref[...], sin_ref[...]
    gq = gq_ref[...] * q_scale
    gk = gk_ref[...]
    hd = A_HEAD_DIM
    for j in range(A_HEADS + A_KV_HEADS):
        t = acc[:, j * hd:(j + 1) * hd]
        y = _rope(_rms(t, gq if j < A_HEADS else gk), cos, sin)
        o_ref[:, j * hd:(j + 1) * hd] = y.astype(BF16)
    v0 = (A_HEADS + A_KV_HEADS) * hd
    o_ref[:, v0:] = acc[:, v0:].astype(BF16)


def _qkv_a_call(xa, mod, layer, w, gq, gk, cos, sin, n_lat):
    bsz, t, d = xa.shape
    n_out = w.shape[1]
    tm = ROW_TILE
    kern = functools.partial(_qkv_a_kernel, n_lat_tiles=n_lat // tm, ctx_row=bsz,
                             q_scale=A_HEAD_DIM ** -0.5 * LOG2E)
    return pl.pallas_call(
        kern,
        grid=(bsz, t // tm),
        in_specs=[
            pl.BlockSpec((None, tm, d), lambda b, i: (b, i, 0)),
            pl.BlockSpec((None, MOD_ROWS, 6 * d), lambda b, i: (layer, 0, 0)),
            _resident((d, n_out), lambda b, i: (0, 0)),
            pl.BlockSpec((1, A_HEAD_DIM), lambda b, i: (0, 0)),
            pl.BlockSpec((1, A_HEAD_DIM), lambda b, i: (0, 0)),
            pl.BlockSpec((tm, LANES), lambda b, i: (i, 0)),
            pl.BlockSpec((tm, LANES), lambda b, i: (i, 0)),
        ],
        out_specs=pl.BlockSpec((None, tm, n_out), lambda b, i: (b, i, 0)),
        out_shape=jax.ShapeDtypeStruct((bsz, t, n_out), BF16),
        compiler_params=_cparams(2, 32 * 1024 * 1024),
        name="qkv_gqa",
    )(xa, mod, w, gq, gk, cos, sin)


def _qkv_b_kernel(x_ref, mod_ref, w_ref, o_ref, *, n_lat_tiles, ctx_row, q_scale):
    d = x_ref.shape[-1]
    row = _mod_row(n_lat_tiles, ctx_row)
    h = _modulated(x_ref, mod_ref, row, 0, d)
    acc = jnp.dot(h, w_ref[...], preferred_element_type=F32)
    o_ref[:, :d] = (acc[:, :d] * q_scale).astype(BF16)
    o_ref[:, d:] = acc[:, d:].astype(BF16)


def _qkv_b_call(xa, mod, layer, w, n_lat):
    bsz, t, d = xa.shape
    n_out = w.shape[1]
    tm = ROW_TILE
    kern = functools.partial(_qkv_b_kernel, n_lat_tiles=n_lat // tm, ctx_row=bsz,
                             q_scale=B_HEAD_DIM ** -0.5 * LOG2E)
    return pl.pallas_call(
        kern,
        grid=(bsz, t // tm),
        in_specs=[
            pl.BlockSpec((None, tm, d), lambda b, i: (b, i, 0)),
            pl.BlockSpec((None, MOD_ROWS, 6 * d), lambda b, i: (layer, 0, 0)),
            _resident((d, n_out), lambda b, i: (0, 0)),
        ],
        out_specs=pl.BlockSpec((None, tm, n_out), lambda b, i: (b, i, 0)),
        out_shape=jax.ShapeDtypeStruct((bsz, t, n_out), BF16),
        compiler_params=_cparams(2, 40 * 1024 * 1024),
        name="qkv_na",
    )(xa, mod, w)


def _proj_c_kernel(x_ref, mod_ref, wd_ref, gq_ref, gkv_ref, wuq_ref, wukv_ref, cos_ref, sin_ref,
                   q_ref, k_ref, v_ref, *, n_lat_tiles, ctx_row, q_scale):
    d = x_ref.shape[-1]
    row = _mod_row(n_lat_tiles, ctx_row)
    h = _modulated(x_ref, mod_ref, row, 0, d)
    cos, sin = cos_ref[...], sin_ref[...]
    dn = jnp.dot(h, wd_ref[...], preferred_element_type=F32)
    r0, r1 = C_Q_RANK, C_Q_RANK + C_KV_RANK
    ql = _rms(dn[:, :r0], gq_ref[...]).astype(BF16)
    kvl = _rms(dn[:, r0:r1], gkv_ref[...]).astype(BF16)
    kr = _rope(dn[:, r1:], cos, sin).astype(BF16)
    q = jnp.dot(ql, wuq_ref[...], preferred_element_type=F32)
    kv = jnp.dot(kvl, wukv_ref[...], preferred_element_type=F32)
    w2 = 2 * LANES
    for hh in range(C_HEADS):
        q_ref[:, hh * w2:hh * w2 + LANES] = (q[:, hh * w2:hh * w2 + LANES] * q_scale).astype(BF16)
        qr = _rope(q[:, hh * w2 + LANES:(hh + 1) * w2], cos, sin) * q_scale
        q_ref[:, hh * w2 + LANES:(hh + 1) * w2] = qr.astype(BF16)
        k_ref[:, hh * w2:hh * w2 + LANES] = kv[:, hh * w2:hh * w2 + LANES].astype(BF16)
        k_ref[:, hh * w2 + LANES:(hh + 1) * w2] = kr
        v_ref[:, hh * LANES:(hh + 1) * LANES] = kv[:, hh * w2 + LANES:(hh + 1) * w2].astype(BF16)


def _proj_c_call(xa, mod, layer, wd, gq, gkv, wuq, wukv, cos, sin, n_lat):
    bsz, t, d = xa.shape
    tm = ROW_TILE
    kern = functools.partial(_proj_c_kernel, n_lat_tiles=n_lat // tm, ctx_row=bsz,
                             q_scale=(C_NOPE + C_ROPE) ** -0.5 * LOG2E)
    const = lambda b, i: (0, 0)
    qk_w = C_HEADS * 2 * LANES
    return pl.pallas_call(
        kern,
        grid=(bsz, t // tm),
        in_specs=[
            pl.BlockSpec((None, tm, d), lambda b, i: (b, i, 0)),
            pl.BlockSpec((None, MOD_ROWS, 6 * d), lambda b, i: (layer, 0, 0)),
            _resident(wd.shape, const),
            pl.BlockSpec(gq.shape, const),
            pl.BlockSpec(gkv.shape, const),
            _resident(wuq.shape, const),
            _resident(wukv.shape, const),
            pl.BlockSpec((tm, LANES), lambda b, i: (i, 0)),
            pl.BlockSpec((tm, LANES), lambda b, i: (i, 0)),
        ],
        out_specs=[
            pl.BlockSpec((None, tm, qk_w), lambda b, i: (b, i, 0)),
            pl.BlockSpec((None, tm, qk_w), lambda b, i: (b, i, 0)),
            pl.BlockSpec((None, tm, C_HEADS * C_V), lambda b, i: (b, i, 0)),
        ],
        out_shape=[
            jax.ShapeDtypeStruct((bsz, t, qk_w), BF16),
            jax.ShapeDtypeStruct((bsz, t, qk_w), BF16),
            jax.ShapeDtypeStruct((bsz, t, C_HEADS * C_V), BF16),
        ],
        compiler_params=_cparams(2, 40 * 1024 * 1024),
        name="proj_mla",
    )(xa, mod, wd, gq, gkv, wuq, wukv, cos, sin)


def _pair_masks(shape):
    lane = lax.broadcasted_iota(jnp.int32, shape, 1)
    return lane < (LANES // 2)


def _dot_nt(a, b):
    return lax.dot_general(a, b, (((1,), (1,)), ((), ())), preferred_element_type=F32)


def _flash_kernel(q_ref, k_ref, v_ref, o_ref, *, n_sub, n_heads, dk, dv, tq_lat, n_lat_steps, lat_chunks,
                  ctx_rows, ctx_chunks):
    def run(row0, tq, chunks):
        m_rows = n_heads * tq
        rows = pl.ds(row0, tq)
        qs, state = [], []
        for u in range(n_sub):
            q0 = u * n_heads * dk
            qs.append(jnp.concatenate([q_ref[rows, q0 + r * dk:q0 + (r + 1) * dk] for r in range(n_heads)], axis=0))
            state.append((jnp.full((m_rows, 1), NEG_BIG, F32), jnp.zeros((m_rows, 1), F32),
                          jnp.zeros((m_rows, dv), F32)))
        for start, size in chunks:
            for u in range(n_sub):
                m, l, acc = state[u]
                k = k_ref[start:start + size, u * dk:(u + 1) * dk]
                v = v_ref[start:start + size, u * dv:(u + 1) * dv]
                s = _dot_nt(qs[u], k)
                m_new = jnp.maximum(m, jnp.max(s, axis=-1, keepdims=True))
                alpha = jnp.exp2(m - m_new)
                p = jnp.exp2(s - m_new)
                l = alpha * l + jnp.sum(p, axis=-1, keepdims=True)
                acc = alpha * acc + jnp.dot(p.astype(BF16), v, preferred_element_type=F32)
                state[u] = (m_new, l, acc)
        for u in range(n_sub):
            _, l, acc = state[u]
            o = acc / l
            o0 = u * n_heads * dv
            for r in range(n_heads):
                o_ref[rows, o0 + r * dv:o0 + (r + 1) * dv] = o[r * tq:(r + 1) * tq].astype(o_ref.dtype)

    i = pl.program_id(2)

    @pl.when(i < n_lat_steps)
    def _():
        run(pl.multiple_of(i * tq_lat, tq_lat), tq_lat, lat_chunks)

    if ctx_rows is not None:
        @pl.when(i >= n_lat_steps)
        def _():
            run(ctx_rows[0], ctx_rows[1], ctx_chunks)


def _flash_call(q_arr, k_arr, v_arr, *, name, n_groups, n_sub, n_heads, dk, dv, tq, k_col0, v_col0, n_lat,
                lat_chunks, ctx_chunks, out_rows):
    bsz, t, _ = k_arr.shape
    assert n_groups % n_sub == 0 and k_col0 % n_sub == 0 and v_col0 % n_sub == 0 and n_lat % tq == 0
    with_ctx = out_rows > n_lat
    n_lat_steps = n_lat // tq
    kern = functools.partial(_flash_kernel, n_sub=n_sub, n_heads=n_heads, dk=dk, dv=dv, tq_lat=tq,
                             n_lat_steps=n_lat_steps, lat_chunks=lat_chunks,
                             ctx_rows=(n_lat, out_rows - n_lat) if with_ctx else None, ctx_chunks=ctx_chunks)
    return pl.pallas_call(
        kern,
        grid=(bsz, n_groups // n_sub, n_lat_steps + int(with_ctx)),
        in_specs=[
            pl.BlockSpec((None, t, n_sub * n_heads * dk), lambda b, g, i: (b, 0, g)),
            pl.BlockSpec((None, t, n_sub * dk), lambda b, g, i: (b, 0, k_col0 // n_sub + g)),
            pl.BlockSpec((None, t, n_sub * dv), lambda b, g, i: (b, 0, v_col0 // n_sub + g)),
        ],
        out_specs=pl.BlockSpec((None, out_rows, n_sub * n_heads * dv), lambda b, g, i: (b, 0, g)),
        out_shape=jax.ShapeDtypeStruct((bsz, out_rows, n_groups * n_heads * dv), BF16),
        compiler_params=_cparams(3, 48 * 1024 * 1024),
        name=name,
    )(q_arr, k_arr, v_arr)


def _na_kernel(q_ref, k_ref, v_ref, bias_ref, o_ref, *, n_lat, n_ctx, n_groups):
    gi = pl.program_id(2)
    rows = n_groups * NA_Q_ROWS
    tq = NA_Q_ROWS * GRID_W
    tk = NA_K_ROWS * GRID_W
    lo = _pair_masks((tq, LANES))

    def stacked_q(cols):
        qb = q_ref[:, cols]
        zero = jnp.zeros_like(qb)
        return jnp.concatenate([jnp.where(lo, qb, zero), jnp.where(lo, zero, qb)], axis=0)

    def finish(o, l, cols):
        o = o / l
        o_ref[:, cols] = jnp.where(lo, o[:tq], o[tq:]).astype(o_ref.dtype)

    @pl.when(gi < n_groups)
    def _():
        start = jnp.clip(gi * NA_Q_ROWS - B_MAX_KH // 2, 0, rows - NA_K_ROWS) * GRID_W
        start = pl.multiple_of(start, GRID_W)
        for s in range(NA_PAIRS_PER_STEP):
            cols = slice(s * LANES, (s + 1) * LANES)
            q = stacked_q(cols)
            kw = k_ref[pl.ds(start, tk), cols]
            vw = v_ref[pl.ds(start, tk), cols]
            kc = k_ref[n_lat:n_lat + n_ctx, cols]
            vc = v_ref[n_lat:n_lat + n_ctx, cols]
            bias = bias_ref[2 * s:2 * s + 2].reshape(2 * tq, tk)
            s_w = _dot_nt(q, kw) + bias
            s_c = _dot_nt(q, kc)
            m = jnp.maximum(jnp.max(s_w, axis=-1, keepdims=True), jnp.max(s_c, axis=-1, keepdims=True))
            p_w = jnp.exp2(s_w - m)
            p_c = jnp.exp2(s_c - m)
            l = jnp.sum(p_w, axis=-1, keepdims=True) + jnp.sum(p_c, axis=-1, keepdims=True)
            o = (jnp.dot(p_w.astype(BF16), vw, preferred_element_type=F32)
                 + jnp.dot(p_c.astype(BF16), vc, preferred_element_type=F32))
            finish(o, l, cols)

    @pl.when(gi >= n_groups)
    def _():
        for s in range(NA_PAIRS_PER_STEP):
            cols = slice(s * LANES, (s + 1) * LANES)
            q = stacked_q(cols)
            s_c = _dot_nt(q, k_ref[n_lat:n_lat + n_ctx, cols])
            p_c = jnp.exp2(s_c - jnp.max(s_c, axis=-1, keepdims=True))
            l = jnp.sum(p_c, axis=-1, keepdims=True)
            finish(jnp.dot(p_c.astype(BF16), v_ref[n_lat:n_lat + n_ctx, cols], preferred_element_type=F32), l, cols)


def _na_call(qkv, bias, n_lat, n_ctx, out_rows):
    bsz, t, w3 = qkv.shape
    d = w3 // 3
    w = NA_PAIRS_PER_STEP * LANES
    n_steps = d // w
    tq = NA_Q_ROWS * GRID_W
    tk = NA_K_ROWS * GRID_W
    n_groups = n_lat // tq
    kern = functools.partial(_na_kernel, n_lat=n_lat, n_ctx=n_ctx, n_groups=n_groups)

    def bias_map(b, p, g):
        case = jnp.where(g == 0, 0, jnp.where(g >= n_groups - 1, 2, 1))
        return (p, case, 0, 0)

    assert n_ctx == tq
    return pl.pallas_call(
        kern,
        grid=(bsz, n_steps, out_rows // tq),
        in_specs=[
            pl.BlockSpec((None, tq, w), lambda b, p, g: (b, g, p)),
            pl.BlockSpec((None, t, w), lambda b, p, g: (b, 0, n_steps + p)),
            pl.BlockSpec((None, t, w), lambda b, p, g: (b, 0, 2 * n_steps + p)),
            pl.BlockSpec((2 * NA_PAIRS_PER_STEP, None, tq, tk), bias_map),
        ],
        out_specs=pl.BlockSpec((None, tq, w), lambda b, p, g: (b, g, p)),
        out_shape=jax.ShapeDtypeStruct((bsz, out_rows, d), BF16),
        compiler_params=_cparams(3, 48 * 1024 * 1024),
        name="attn_na",
    )(qkv, qkv, qkv, bias)


def _post_kernel(o_ref, x_ref, mod_ref, wo_ref, wg_ref, wu_ref, wd_ref, lng_ref, lnb_ref, out_ref,
                 *, n_lat_tiles, ctx_row, alpha, ff_chunk):
    d = x_ref.shape[-1]
    row = _mod_row(n_lat_tiles, ctx_row)
    g1 = _mod_vec(mod_ref, row, 2, d)
    sh2 = _mod_vec(mod_ref, row, 3, d)
    sc2 = _mod_vec(mod_ref, row, 4, d)
    g2 = _mod_vec(mod_ref, row, 5, d)
    y = jnp.dot(o_ref[...], wo_ref[...], preferred_element_type=F32)
    x1 = _layer_norm(alpha * x_ref[...] + g1 * y, lng_ref[0:1, :], lnb_ref[0:1, :])
    h = (x1 * (1.0 + sc2) + sh2).astype(BF16)
    d_ff = wg_ref.shape[1]
    f = jnp.zeros(x1.shape, F32)
    for c0 in range(0, d_ff, ff_chunk):
        gate = jnp.dot(h, wg_ref[:, c0:c0 + ff_chunk], preferred_element_type=F32)
        up = jnp.dot(h, wu_ref[:, c0:c0 + ff_chunk], preferred_element_type=F32)
        a = (gate / (1.0 + jnp.exp(-gate)) * up).astype(BF16)
        f = f + jnp.dot(a, wd_ref[c0:c0 + ff_chunk, :], preferred_element_type=F32)
    out_ref[...] = _layer_norm(alpha * x1 + g2 * f, lng_ref[1:2, :], lnb_ref[1:2, :])


def _post_call(o, xa, mod, layer, wo, wg, wu, wd, lng, lnb, n_lat, out_rows, alpha):
    bsz, _, d = xa.shape
    tm = ROW_TILE
    d_ff = wg.shape[1]
    kern = functools.partial(_post_kernel, n_lat_tiles=n_lat // tm, ctx_row=bsz, alpha=alpha, ff_chunk=d_ff)
    const = lambda b, i: (0, 0)
    return pl.pallas_call(
        kern,
        grid=(bsz, out_rows // tm),
        in_specs=[
            pl.BlockSpec((None, tm, o.shape[-1]), lambda b, i: (b, i, 0)),
            pl.BlockSpec((None, tm, d), lambda b, i: (b, i, 0)),
            pl.BlockSpec((None, MOD_ROWS, 6 * d), lambda b, i: (layer, 0, 0)),
            _resident(wo.shape, const),
            _resident((d, d_ff), const),
            _resident((d, d_ff), const),
            _resident((d_ff, d), const),
            pl.BlockSpec((None, 2, d), lambda b, i: (layer, 0, 0)),
            pl.BlockSpec((None, 2, d), lambda b, i: (layer, 0, 0)),
        ],
        out_specs=pl.BlockSpec((None, tm, d), lambda b, i: (b, i, 0)),
        out_shape=jax.ShapeDtypeStruct((bsz, out_rows, d), F32),
        compiler_params=_cparams(2, VMEM_LIMIT),
        name="post_ffn",
    )(o, xa, mod, wo, wg, wu, wd, lng, lnb)


def _rope_tables(n_lat, n_ctx, n_freq):
    half = LANES // 2
    t = np.arange(n_lat)
    freqs = ROPE_BASE ** (-np.arange(0, 2 * n_freq, 2, dtype=np.float64) / (2 * n_freq))
    ang = np.concatenate([(t // GRID_W)[:, None] * freqs[None, :], (t % GRID_W)[:, None] * freqs[None, :]], axis=1)
    cos = np.zeros((n_lat + n_ctx, LANES))
    sin = np.zeros((n_lat + n_ctx, LANES))
    w = 2 * n_freq
    cos[:n_lat, :w] = np.cos(ang)
    cos[:n_lat, half:half + w] = np.cos(ang)
    sin[:n_lat, :w] = -np.sin(ang)
    sin[:n_lat, half:half + w] = np.sin(ang)
    cos[n_lat:, :w] = 1.0
    cos[n_lat:, half:half + w] = 1.0
    return jnp.asarray(cos, F32), jnp.asarray(sin, F32)


def _rope_perm(n_freq):
    idx = np.full((LANES,), -1, np.int64)
    f = n_freq
    idx[0:f] = np.arange(0, f)
    idx[f:2 * f] = np.arange(2 * f, 3 * f)
    idx[64:64 + f] = np.arange(f, 2 * f)
    idx[64 + f:64 + 2 * f] = np.arange(3 * f, 4 * f)
    return idx


def _take_cols(w, idx):
    wz = jnp.concatenate([w, jnp.zeros((w.shape[0], 1), w.dtype)], axis=1)
    return wz[:, np.where(idx < 0, w.shape[1], idx)]


def _na_row_cases(rows):
    kh = B_MAX_KH
    i = np.arange(NA_Q_ROWS)[:, None]
    j = np.arange(NA_K_ROWS)[None, :]
    cases = []
    for r0 in (0, NA_Q_ROWS, rows - NA_Q_ROWS):
        start = int(np.clip(r0 - kh // 2, 0, rows - NA_K_ROWS))
        rs = np.clip(r0 + i - kh // 2, 0, rows - kh)
        key_row = start + j
        cases.append(((key_row >= rs) & (key_row < rs + kh), key_row - (r0 + i) + (kh - 1)))
    return np.stack([c[0] for c in cases]), np.stack([c[1] for c in cases])


def _na_bias_kernel(r2_ref, o_ref, *, row_ok, dr_idx):
    n_e = r2_ref.shape[0]
    shape = (GRID_W, LANES)
    c = lax.broadcasted_iota(jnp.int32, shape, 0)
    lane = lax.broadcasted_iota(jnp.int32, shape, 1)
    kc = jnp.where(lane < GRID_W, lane, lane - GRID_W)
    cs = jnp.clip(c - B_KW // 2, 0, GRID_W - B_KW)
    col_ok = (kc >= cs) & (kc < cs + B_KW)
    left = lane < GRID_W
    neg = jnp.full(shape, NEG_BIG, F32)
    x = r2_ref[...]
    tiles = []
    for e in range(n_e):
        t = pltpu.roll(jnp.broadcast_to(x[e:e + 1, :], shape), LANES - (B_KW - 1), 1, stride=1, stride_axis=0)
        tiles.append(jnp.where(col_ok, t * LOG2E, neg))
    for case in range(row_ok.shape[0]):
        for i in range(NA_Q_ROWS):
            for jp in range(NA_K_ROWS // 2):
                ok_l, ok_r = bool(row_ok[case, i, 2 * jp]), bool(row_ok[case, i, 2 * jp + 1])
                e = int(np.clip(dr_idx[case, i, 2 * jp] + 1, 0, n_e - 1))
                if ok_l and ok_r:
                    tile = tiles[e]
                elif ok_l:
                    tile = jnp.where(left, tiles[e], neg)
                elif ok_r:
                    tile = jnp.where(left, neg, tiles[e])
                else:
                    tile = neg
                o_ref[case, i * GRID_W:(i + 1) * GRID_W, jp * LANES:(jp + 1) * LANES] = tile


def _na_bias_tables(rpb, rows):
    n_heads, n_dr, n_dc = rpb.shape
    n_e = n_dr + 1
    r2 = jnp.zeros((n_heads, n_e, LANES), F32)
    r2 = r2.at[:, 1:, :n_dc].set(rpb).at[:, :n_dr, GRID_W:GRID_W + n_dc].set(rpb)
    row_ok, dr_idx = _na_row_cases(rows)
    kern = functools.partial(_na_bias_kernel, row_ok=row_ok, dr_idx=dr_idx)
    tq, tk = NA_Q_ROWS * GRID_W, NA_K_ROWS * GRID_W
    return pl.pallas_call(
        kern,
        grid=(n_heads,),
        in_specs=[pl.BlockSpec((None, n_e, LANES), lambda h: (h, 0, 0))],
        out_specs=pl.BlockSpec((None, 3, tq, tk), lambda h: (h, 0, 0, 0)),
        out_shape=jax.ShapeDtypeStruct((n_heads, 3, tq, tk), F32),
        compiler_params=_cparams(1, 32 * 1024 * 1024),
        name="na_bias",
    )(r2)


def kernel(x, c, ctx, c_ctx, w_ada, b_ada, ln_g, ln_b, w_ffn_gate, w_ffn_up, w_ffn_down, a_w_qkv, a_q_gain, a_k_gain, a_w_o, b_w_qkv, b_rpb, b_w_o, c_w_dqkv, c_q_a_gain, c_kv_a_gain, c_w_uq, c_w_ukv, c_w_o):
    bsz, n_lat, d = x.shape
    n_ctx = ctx.shape[1]
    t = n_lat + n_ctx
    depth = w_ada.shape[0]
    rows = n_lat // GRID_W
    assert n_lat % ROW_TILE == 0 and n_ctx == ROW_TILE and bsz < MOD_ROWS
    assert n_lat % (NA_Q_ROWS * GRID_W) == 0 and rows >= NA_K_ROWS
    assert a_w_qkv.shape[2] == (A_HEADS + 2 * A_KV_HEADS) * A_HEAD_DIM and b_rpb.shape[1] == B_HEADS
    assert c_w_dqkv.shape[2] == C_Q_RANK + C_KV_RANK + C_ROPE and d == B_HEADS * B_HEAD_DIM
    alpha = (2.0 * depth) ** 0.25

    xa = jnp.concatenate([x, ctx], axis=1)
    cond = jnp.zeros((MOD_ROWS, d), F32).at[:bsz].set(c).at[bsz].set(c_ctx)
    mod = _ada_call(cond, w_ada, b_ada)

    cos_a, sin_a = _rope_tables(n_lat, n_ctx, A_HEAD_DIM // 4)
    cos_c, sin_c = _rope_tables(n_lat, n_ctx, C_ROPE // 4)
    perm_a = _rope_perm(A_HEAD_DIM // 4)
    perm_c = _rope_perm(C_ROPE // 4)

    ctx_chunks = ((n_lat, n_ctx),)
    lat_chunks = tuple((s, 1024) for s in range(0, n_lat, 1024)) + ctx_chunks

    for i in range(depth):
        last = i == depth - 1
        out_rows = n_lat if last else t
        kind, j = i % 3, i // 3
        if kind == 0:
            hd = A_HEAD_DIM
            nqk = A_HEADS + A_KV_HEADS
            col_idx = np.concatenate([h * hd + perm_a for h in range(nqk)]
                                     + [np.arange(nqk * hd, (nqk + A_KV_HEADS) * hd)])
            w = a_w_qkv[j][:, col_idx].astype(BF16)
            qkv = _qkv_a_call(xa, mod, i, w, a_q_gain[j][perm_a][None, :], a_k_gain[j][perm_a][None, :],
                              cos_a, sin_a, n_lat)
            o = _flash_call(qkv, qkv, qkv, name="attn_gqa", n_groups=A_KV_HEADS, n_sub=1,
                            n_heads=A_HEADS // A_KV_HEADS, dk=hd, dv=hd, tq=256, k_col0=A_HEADS,
                            v_col0=A_HEADS + A_KV_HEADS, n_lat=n_lat, lat_chunks=lat_chunks,
                            ctx_chunks=ctx_chunks, out_rows=out_rows)
            wo = a_w_o[j]
        elif kind == 1:
            qkv = _qkv_b_call(xa, mod, i, b_w_qkv[j].astype(BF16), n_lat)
            o = _na_call(qkv, _na_bias_tables(b_rpb[j], rows), n_lat, n_ctx, out_rows)
            wo = b_w_o[j]
        else:
            r1 = C_Q_RANK + C_KV_RANK
            wd_idx = np.concatenate([np.arange(r1), np.where(perm_c < 0, -1, r1 + perm_c)])
            wdn = _take_cols(c_w_dqkv[j], wd_idx).astype(BF16)
            hq = C_NOPE + C_ROPE
            uq_idx = np.concatenate([np.concatenate([h * hq + np.arange(C_NOPE),
                                                     np.where(perm_c < 0, -1, h * hq + C_NOPE + perm_c)])
                                     for h in range(C_HEADS)])
            wuq = _take_cols(c_w_uq[j], uq_idx).astype(BF16)
            qc, kc, vc = _proj_c_call(xa, mod, i, wdn, c_q_a_gain[j][None, :], c_kv_a_gain[j][None, :], wuq,
                                      c_w_ukv[j].astype(BF16), cos_c, sin_c, n_lat)
            o = _flash_call(qc, kc, vc, name="attn_mla", n_groups=C_HEADS, n_sub=1, n_heads=1, dk=2 * LANES,
                            dv=C_V, tq=1024, k_col0=0, v_col0=0, n_lat=n_lat, lat_chunks=lat_chunks,
                            ctx_chunks=ctx_chunks, out_rows=out_rows)
            wo = c_w_o[j]
        xa = _post_call(o, xa, mod, i, wo.astype(BF16), w_ffn_gate[i].astype(BF16), w_ffn_up[i].astype(BF16),
                        w_ffn_down[i].astype(BF16), ln_g, ln_b, n_lat, out_rows, alpha)
    return xa
```

```python
import functools
import math

import numpy as np
import jax
import jax.numpy as jnp
from jax import lax
from jax.experimental import pallas as pl
from jax.experimental.pallas import tpu as pltpu

GRID_W = 64
A_HEADS, A_KV_HEADS, A_HEAD_DIM = 8, 2, 128
B_HEADS, B_HEAD_DIM, B_MAX_KH, B_KW = 16, 64, 8, 16
C_HEADS, C_NOPE, C_ROPE, C_V, C_Q_RANK, C_KV_RANK = 8, 128, 64, 128, 384, 256
ROPE_BASE = 10000.0
RMS_EPS = 1e-6
LN_EPS = 1e-5
LOG2E = math.log2(math.e)
NEG_BIG = -1e30

LANES = 128
V7X_VMEM_BYTES = 64 * 1024 * 1024
VMEM_LIMIT = 56 * 1024 * 1024

ROW_TILE = 256
MOD_ROWS = 8
NA_Q_ROWS = 4
NA_K_ROWS = 12
NA_PAIRS_PER_STEP = 4

BF16 = jnp.bfloat16
F32 = jnp.float32


def _cparams(n_axes, vmem=None):
    return pltpu.CompilerParams(dimension_semantics=("arbitrary",) * n_axes, vmem_limit_bytes=vmem)


def _resident(shape, index_map):
    return pl.BlockSpec(shape, index_map, pipeline_mode=pl.Buffered(1))


def _ada_kernel(c_ref, w_ref, b_ref, o_ref):
    c = c_ref[...]
    cs = c / (1.0 + jnp.exp(-c))
    o_ref[...] = jnp.dot(cs, w_ref[...], preferred_element_type=F32) + b_ref[...]


def _ada_call(cond, w_ada, b_ada):
    depth, d, n6 = w_ada.shape
    tn = n6 // 4
    return pl.pallas_call(
        _ada_kernel,
        grid=(depth, n6 // tn),
        in_specs=[
            pl.BlockSpec((MOD_ROWS, d), lambda l, j: (0, 0)),
            pl.BlockSpec((None, d, tn), lambda l, j: (l, 0, j)),
            pl.BlockSpec((None, 1, tn), lambda l, j: (l, 0, j)),
        ],
        out_specs=pl.BlockSpec((None, MOD_ROWS, tn), lambda l, j: (l, 0, j)),
        out_shape=jax.ShapeDtypeStruct((depth, MOD_ROWS, n6), F32),
        compiler_params=_cparams(2, 40 * 1024 * 1024),
        name="ada_mod",
    )(cond, w_ada, b_ada.reshape(depth, 1, n6))


def _mod_row(n_lat_tiles, ctx_row):
    b, i = pl.program_id(0), pl.program_id(1)
    return jnp.where(i < n_lat_tiles, b, ctx_row)


def _mod_vec(mod_ref, row, k, d):
    return mod_ref[pl.ds(row, 1), k * d:(k + 1) * d]


def _modulated(x_ref, mod_ref, row, k_shift, d):
    sh = _mod_vec(mod_ref, row, k_shift, d)
    sc = _mod_vec(mod_ref, row, k_shift + 1, d)
    return (x_ref[...] * (1.0 + sc) + sh).astype(BF16)


def _rms(t, g):
    return t * lax.rsqrt(jnp.mean(t * t, axis=-1, keepdims=True) + RMS_EPS) * g


def _rope(t, cos, sin):
    return t * cos + pltpu.roll(t, LANES // 2, 1) * sin


def _layer_norm(t, g, b):
    mu = jnp.mean(t, axis=-1, keepdims=True)
    c = t - mu
    var = jnp.mean(c * c, axis=-1, keepdims=True)
    return c * lax.rsqrt(var + LN_EPS) * g + b


def _qkv_a_kernel(x_ref, mod_ref, w_ref, gq_ref, gk_ref, cos_ref, sin_ref, o_ref, *, n_lat_tiles, ctx_row, q_scale):
    d = x_ref.shape[-1]
    row = _mod_row(n_lat_tiles, ctx_row)
    h = _modulated(x_ref, mod_ref, row, 0, d)
    acc = jnp.dot(h, w_ref[...], preferred_element_type=F32)
    cos, sin = cos_ref[...], sin_ref[...]
    gq = gq_ref[...] * q_scale
    gk = gk_ref[...]
    hd = A_HEAD_DIM
    for j in range(A_HEADS + A_KV_HEADS):
        t = acc[:, j * hd:(j + 1) * hd]
        y = _rope(_rms(t, gq if j < A_HEADS else gk), cos, sin)
        o_ref[:, j * hd:(j + 1) * hd] = y.astype(BF16)
    v0 = (A_HEADS + A_KV_HEADS) * hd
    o_ref[:, v0:] = acc[:, v0:].astype(BF16)


def _qkv_a_call(xa, mod, layer, w, gq, gk, cos, sin, n_lat):
    bsz, t, d = xa.shape
    n_out = w.shape[1]
    tm = ROW_TILE
    kern = functools.partial(_qkv_a_kernel, n_lat_tiles=n_lat // tm, ctx_row=bsz,
                             q_scale=A_HEAD_DIM ** -0.5 * LOG2E)
    return pl.pallas_call(
        kern,
        grid=(bsz, t // tm),
        in_specs=[
            pl.BlockSpec((None, tm, d), lambda b, i: (b, i, 0)),
            pl.BlockSpec((None, MOD_ROWS, 6 * d), lambda b, i: (layer, 0, 0)),
            _resident((d, n_out), lambda b, i: (0, 0)),
            pl.BlockSpec((1, A_HEAD_DIM), lambda b, i: (0, 0)),
            pl.BlockSpec((1, A_HEAD_DIM), lambda b, i: (0, 0)),
            pl.BlockSpec((tm, LANES), lambda b, i: (i, 0)),
            pl.BlockSpec((tm, LANES), lambda b, i: (i, 0)),
        ],
        out_specs=pl.BlockSpec((None, tm, n_out), lambda b, i: (b, i, 0)),
        out_shape=jax.ShapeDtypeStruct((bsz, t, n_out), BF16),
        compiler_params=_cparams(2, 32 * 1024 * 1024),
        name="qkv_gqa",
    )(xa, mod, w, gq, gk, cos, sin)


def _qkv_b_kernel(x_ref, mod_ref, w_ref, o_ref, *, n_lat_tiles, ctx_row, q_scale):
    d = x_ref.shape[-1]
    row = _mod_row(n_lat_tiles, ctx_row)
    h = _modulated(x_ref, mod_ref, row, 0, d)
    acc = jnp.dot(h, w_ref[...], preferred_element_type=F32)
    o_ref[:, :d] = (acc[:, :d] * q_scale).astype(BF16)
    o_ref[:, d:] = acc[:, d:].astype(BF16)


def _qkv_b_call(xa, mod, layer, w, n_lat):
    bsz, t, d = xa.shape
    n_out = w.shape[1]
    tm = ROW_TILE
    kern = functools.partial(_qkv_b_kernel, n_lat_tiles=n_lat // tm, ctx_row=bsz,
                             q_scale=B_HEAD_DIM ** -0.5 * LOG2E)
    return pl.pallas_call(
        kern,
        grid=(bsz, t // tm),
        in_specs=[
            pl.BlockSpec((None, tm, d), lambda b, i: (b, i, 0)),
            pl.BlockSpec((None, MOD_ROWS, 6 * d), lambda b, i: (layer, 0, 0)),
            _resident((d, n_out), lambda b, i: (0, 0)),
        ],
        out_specs=pl.BlockSpec((None, tm, n_out), lambda b, i: (b, i, 0)),
        out_shape=jax.ShapeDtypeStruct((bsz, t, n_out), BF16),
        compiler_params=_cparams(2, 40 * 1024 * 1024),
        name="qkv_na",
    )(xa, mod, w)


def _proj_c_kernel(x_ref, mod_ref, wd_ref, gq_ref, gkv_ref, wuq_ref, wukv_ref, cos_ref, sin_ref,
                   q_ref, k_ref, v_ref, *, n_lat_tiles, ctx_row, q_scale):
    d = x_ref.shape[-1]
    row = _mod_row(n_lat_tiles, ctx_row)
    h = _modulated(x_ref, mod_ref, row, 0, d)
    cos, sin = cos_ref[...], sin_ref[...]
    dn = jnp.dot(h, wd_ref[...], preferred_element_type=F32)
    r0, r1 = C_Q_RANK, C_Q_RANK + C_KV_RANK
    ql = _rms(dn[:, :r0], gq_ref[...]).astype(BF16)
    kvl = _rms(dn[:, r0:r1], gkv_ref[...]).astype(BF16)
    kr = _rope(dn[:, r1:], cos, sin).astype(BF16)
    q = jnp.dot(ql, wuq_ref[...], preferred_element_type=F32)
    kv = jnp.dot(kvl, wukv_ref[...], preferred_element_type=F32)
    w2 = 2 * LANES
    for hh in range(C_HEADS):
        q_ref[:, hh * w2:hh * w2 + LANES] = (q[:, hh * w2:hh * w2 + LANES] * q_scale).astype(BF16)
        qr = _rope(q[:, hh * w2 + LANES:(hh + 1) * w2], cos, sin) * q_scale
        q_ref[:, hh * w2 + LANES:(hh + 1) * w2] = qr.astype(BF16)
        k_ref[:, hh * w2:hh * w2 + LANES] = kv[:, hh * w2:hh * w2 + LANES].astype(BF16)
        k_ref[:, hh * w2 + LANES:(hh + 1) * w2] = kr
        v_ref[:, hh * LANES:(hh + 1) * LANES] = kv[:, hh * w2 + LANES:(hh + 1) * w2].astype(BF16)


def _proj_c_call(xa, mod, layer, wd, gq, gkv, wuq, wukv, cos, sin, n_lat):
    bsz, t, d = xa.shape
    tm = ROW_TILE
    kern = functools.partial(_proj_c_kernel, n_lat_tiles=n_lat // tm, ctx_row=bsz,
                             q_scale=(C_NOPE + C_ROPE) ** -0.5 * LOG2E)
    const = lambda b, i: (0, 0)
    qk_w = C_HEADS * 2 * LANES
    return pl.pallas_call(
        kern,
        grid=(bsz, t // tm),
        in_specs=[
            pl.BlockSpec((None, tm, d), lambda b, i: (b, i, 0)),
            pl.BlockSpec((None, MOD_ROWS, 6 * d), lambda b, i: (layer, 0, 0)),
            _resident(wd.shape, const),
            pl.BlockSpec(gq.shape, const),
            pl.BlockSpec(gkv.shape, const),
            _resident(wuq.shape, const),
            _resident(wukv.shape, const),
            pl.BlockSpec((tm, LANES), lambda b, i: (i, 0)),
            pl.BlockSpec((tm, LANES), lambda b, i: (i, 0)),
        ],
        out_specs=[
            pl.BlockSpec((None, tm, qk_w), lambda b, i: (b, i, 0)),
            pl.BlockSpec((None, tm, qk_w), lambda b, i: (b, i, 0)),
            pl.BlockSpec((None, tm, C_HEADS * C_V), lambda b, i: (b, i, 0)),
        ],
        out_shape=[
            jax.ShapeDtypeStruct((bsz, t, qk_w), BF16),
            jax.ShapeDtypeStruct((bsz, t, qk_w), BF16),
            jax.ShapeDtypeStruct((bsz, t, C_HEADS * C_V), BF16),
        ],
        compiler_params=_cparams(2, 40 * 1024 * 1024),
        name="proj_mla",
    )(xa, mod, wd, gq, gkv, wuq, wukv, cos, sin)


def _pair_masks(shape):
    lane = lax.broadcasted_iota(jnp.int32, shape, 1)
    return lane < (LANES // 2)


def _dot_nt(a, b):
    return lax.dot_general(a, b, (((1,), (1,)), ((), ())), preferred_element_type=F32)


def _flash_kernel(q_ref, k_ref, v_ref, o_ref, *, n_sub, n_split, n_heads, dk, dv, tq_lat, n_lat_steps, lat_chunks,
                  ctx_rows, ctx_chunks):
    def run(row0, tq, chunks):
        m_rows = n_heads * tq // n_split
        rows = pl.ds(row0, tq)
        chains, qs, state = [], [], []
        for u in range(n_sub):
            q0 = u * n_heads * dk
            q = jnp.concatenate([q_ref[rows, q0 + r * dk:q0 + (r + 1) * dk] for r in range(n_heads)], axis=0)
            for h in range(n_split):
                chains.append(u)
                qs.append(q[h * m_rows:(h + 1) * m_rows])
                state.append((jnp.full((m_rows, 1), NEG_BIG, F32), jnp.zeros((m_rows, dv + LANES), F32)))
        for start, size in chunks:
            ones = jnp.ones((size, LANES), BF16)
            scores = [_dot_nt(qs[c], k_ref[start:start + size, u * dk:(u + 1) * dk])
                      for c, u in enumerate(chains)]
            for c, u in enumerate(chains):
                m, acc = state[c]
                v1 = jnp.concatenate([v_ref[start:start + size, u * dv:(u + 1) * dv], ones], axis=1)
                m_new = jnp.maximum(m, jnp.max(scores[c], axis=-1, keepdims=True))
                alpha = jnp.exp2(m - m_new)
                p = jnp.exp2(scores[c] - m_new)
                acc = alpha * acc + jnp.dot(p.astype(BF16), v1, preferred_element_type=F32)
                state[c] = (m_new, acc)
        for u in range(n_sub):
            accs = [state[u * n_split + h][1] for h in range(n_split)]
            acc = accs[0] if n_split == 1 else jnp.concatenate(accs, axis=0)
            o = acc[:, :dv] / acc[:, dv:]
            o0 = u * n_heads * dv
            for r in range(n_heads):
                o_ref[rows, o0 + r * dv:o0 + (r + 1) * dv] = o[r * tq:(r + 1) * tq].astype(o_ref.dtype)

    i = pl.program_id(2)

    @pl.when(i < n_lat_steps)
    def _():
        run(pl.multiple_of(i * tq_lat, tq_lat), tq_lat, lat_chunks)

    if ctx_rows is not None:
        @pl.when(i >= n_lat_steps)
        def _():
            run(ctx_rows[0], ctx_rows[1], ctx_chunks)


def _flash_call(q_arr, k_arr, v_arr, *, name, n_groups, n_sub, n_split, n_heads, dk, dv, tq, k_col0, v_col0, n_lat,
                lat_chunks, ctx_chunks, out_rows):
    bsz, t, _ = k_arr.shape
    assert n_groups % n_sub == 0 and k_col0 % n_sub == 0 and v_col0 % n_sub == 0 and n_lat % tq == 0
    with_ctx = out_rows > n_lat
    n_lat_steps = n_lat // tq
    kern = functools.partial(_flash_kernel, n_sub=n_sub, n_split=n_split, n_heads=n_heads, dk=dk, dv=dv, tq_lat=tq,
                             n_lat_steps=n_lat_steps, lat_chunks=lat_chunks,
                             ctx_rows=(n_lat, out_rows - n_lat) if with_ctx else None, ctx_chunks=ctx_chunks)
    return pl.pallas_call(
        kern,
        grid=(bsz, n_groups // n_sub, n_lat_steps + int(with_ctx)),
        in_specs=[
            pl.BlockSpec((None, t, n_sub * n_heads * dk), lambda b, g, i: (b, 0, g)),
            pl.BlockSpec((None, t, n_sub * dk), lambda b, g, i: (b, 0, k_col0 // n_sub + g)),
            pl.BlockSpec((None, t, n_sub * dv), lambda b, g, i: (b, 0, v_col0 // n_sub + g)),
        ],
        out_specs=pl.BlockSpec((None, out_rows, n_sub * n_heads * dv), lambda b, g, i: (b, 0, g)),
        out_shape=jax.ShapeDtypeStruct((bsz, out_rows, n_groups * n_heads * dv), BF16),
        compiler_params=_cparams(3, 48 * 1024 * 1024),
        name=name,
    )(q_arr, k_arr, v_arr)


def _na_kernel(q_ref, k_ref, v_ref, bias_ref, o_ref, *, n_lat, n_ctx, n_groups):
    gi = pl.program_id(2)
    rows = n_groups * NA_Q_ROWS
    tq = NA_Q_ROWS * GRID_W
    tk = NA_K_ROWS * GRID_W
    lo = _pair_masks((tq, LANES))

    def stacked_q(cols):
        qb = q_ref[:, cols]
        zero = jnp.zeros_like(qb)
        return jnp.concatenate([jnp.where(lo, qb, zero), jnp.where(lo, zero, qb)], axis=0)

    def with_ones(v):
        return jnp.concatenate([v, jnp.ones(v.shape, v.dtype)], axis=1)

    def finish(o1, cols):
        o = o1[:, :LANES] / o1[:, LANES:]
        o_ref[:, cols] = jnp.where(lo, o[:tq], o[tq:]).astype(o_ref.dtype)

    @pl.when(gi < n_groups)
    def _():
        start = jnp.clip(gi * NA_Q_ROWS - B_MAX_KH // 2, 0, rows - NA_K_ROWS) * GRID_W
        start = pl.multiple_of(start, GRID_W)
        scores = []
        for s in range(NA_PAIRS_PER_STEP):
            cols = slice(s * LANES, (s + 1) * LANES)
            kk = jnp.concatenate([k_ref[pl.ds(start, tk), cols], k_ref[n_lat:n_lat + n_ctx, cols]], axis=0)
            scores.append(_dot_nt(stacked_q(cols), kk))
        for s in range(NA_PAIRS_PER_STEP):
            cols = slice(s * LANES, (s + 1) * LANES)
            vv = with_ones(jnp.concatenate([v_ref[pl.ds(start, tk), cols], v_ref[n_lat:n_lat + n_ctx, cols]], axis=0))
            bias = bias_ref[2 * s:2 * s + 2].reshape(2 * tq, tk)
            sc = jnp.concatenate([scores[s][:, :tk] + bias, scores[s][:, tk:]], axis=1)
            p = jnp.exp2(sc - jnp.max(sc, axis=-1, keepdims=True))
            finish(jnp.dot(p.astype(BF16), vv, preferred_element_type=F32), cols)

    @pl.when(gi >= n_groups)
    def _():
        for s in range(NA_PAIRS_PER_STEP):
            cols = slice(s * LANES, (s + 1) * LANES)
            q = stacked_q(cols)
            s_c = _dot_nt(q, k_ref[n_lat:n_lat + n_ctx, cols])
            p_c = jnp.exp2(s_c - jnp.max(s_c, axis=-1, keepdims=True))
            vc = with_ones(v_ref[n_lat:n_lat + n_ctx, cols])
            finish(jnp.dot(p_c.astype(BF16), vc, preferred_element_type=F32), cols)


def _na_call(qkv, bias, n_lat, n_ctx, out_rows):
    bsz, t, w3 = qkv.shape
    d = w3 // 3
    w = NA_PAIRS_PER_STEP * LANES
    n_steps = d // w
    tq = NA_Q_ROWS * GRID_W
    tk = NA_K_ROWS * GRID_W
    n_groups = n_lat // tq
    kern = functools.partial(_na_kernel, n_lat=n_lat, n_ctx=n_ctx, n_groups=n_groups)

    def bias_map(b, p, g):
        case = jnp.where(g == 0, 0, jnp.where(g >= n_groups - 1, 2, 1))
        return (p, case, 0, 0)

    assert n_ctx == tq
    return pl.pallas_call(
        kern,
        grid=(bsz, n_steps, out_rows // tq),
        in_specs=[
            pl.BlockSpec((None, tq, w), lambda b, p, g: (b, g, p)),
            pl.BlockSpec((None, t, w), lambda b, p, g: (b, 0, n_steps + p)),
            pl.BlockSpec((None, t, w), lambda b, p, g: (b, 0, 2 * n_steps + p)),
            pl.BlockSpec((2 * NA_PAIRS_PER_STEP, None, tq, tk), bias_map),
        ],
        out_specs=pl.BlockSpec((None, tq, w), lambda b, p, g: (b, g, p)),
        out_shape=jax.ShapeDtypeStruct((bsz, out_rows, d), BF16),
        compiler_params=_cparams(3, 48 * 1024 * 1024),
        name="attn_na",
    )(qkv, qkv, qkv, bias)


def _post_kernel(o_ref, x_ref, mod_ref, wo_ref, wg_ref, wu_ref, wd_ref, lng_ref, lnb_ref, out_ref,
                 *, n_lat_tiles, ctx_row, alpha, ff_chunk):
    d = x_ref.shape[-1]
    row = _mod_row(n_lat_tiles, ctx_row)
    g1 = _mod_vec(mod_ref, row, 2, d)
    sh2 = _mod_vec(mod_ref, row, 3, d)
    sc2 = _mod_vec(mod_ref, row, 4, d)
    g2 = _mod_vec(mod_ref, row, 5, d)
    y = jnp.dot(o_ref[...], wo_ref[...], preferred_element_type=F32)
    x1 = _layer_norm(alpha * x_ref[...] + g1 * y, lng_ref[0:1, :], lnb_ref[0:1, :])
    h = (x1 * (1.0 + sc2) + sh2).astype(BF16)
    d_ff = wg_ref.shape[1]
    f = jnp.zeros(x1.shape, F32)
    for c0 in range(0, d_ff, ff_chunk):
        gate = jnp.dot(h, wg_ref[:, c0:c0 + ff_chunk], preferred_element_type=F32)
        up = jnp.dot(h, wu_ref[:, c0:c0 + ff_chunk], preferred_element_type=F32)
        a = (gate / (1.0 + jnp.exp(-gate)) * up).astype(BF16)
        f = f + jnp.dot(a, wd_ref[c0:c0 + ff_chunk, :], preferred_element_type=F32)
    out_ref[...] = _layer_norm(alpha * x1 + g2 * f, lng_ref[1:2, :], lnb_ref[1:2, :])


def _post_call(o, xa, mod, layer, wo, wg, wu, wd, lng, lnb, n_lat, out_rows, alpha):
    bsz, _, d = xa.shape
    tm = ROW_TILE
    d_ff = wg.shape[1]
    kern = functools.partial(_post_kernel, n_lat_tiles=n_lat // tm, ctx_row=bsz, alpha=alpha, ff_chunk=d_ff)
    const = lambda b, i: (0, 0)
    return pl.pallas_call(
        kern,
        grid=(bsz, out_rows // tm),
        in_specs=[
            pl.BlockSpec((None, tm, o.shape[-1]), lambda b, i: (b, i, 0)),
            pl.BlockSpec((None, tm, d), lambda b, i: (b, i, 0)),
            pl.BlockSpec((None, MOD_ROWS, 6 * d), lambda b, i: (layer, 0, 0)),
            _resident(wo.shape, const),
            _resident((d, d_ff), const),
            _resident((d, d_ff), const),
            _resident((d_ff, d), const),
            pl.BlockSpec((None, 2, d), lambda b, i: (layer, 0, 0)),
            pl.BlockSpec((None, 2, d), lambda b, i: (layer, 0, 0)),
        ],
        out_specs=pl.BlockSpec((None, tm, d), lambda b, i: (b, i, 0)),
        out_shape=jax.ShapeDtypeStruct((bsz, out_rows, d), F32),
        compiler_params=_cparams(2, VMEM_LIMIT),
        name="post_ffn",
    )(o, xa, mod, wo, wg, wu, wd, lng, lnb)


def _rope_tables(n_lat, n_ctx, n_freq):
    half = LANES // 2
    t = np.arange(n_lat)
    freqs = ROPE_BASE ** (-np.arange(0, 2 * n_freq, 2, dtype=np.float64) / (2 * n_freq))
    ang = np.concatenate([(t // GRID_W)[:, None] * freqs[None, :], (t % GRID_W)[:, None] * freqs[None, :]], axis=1)
    cos = np.zeros((n_lat + n_ctx, LANES))
    sin = np.zeros((n_lat + n_ctx, LANES))
    w = 2 * n_freq
    cos[:n_lat, :w] = np.cos(ang)
    cos[:n_lat, half:half + w] = np.cos(ang)
    sin[:n_lat, :w] = -np.sin(ang)
    sin[:n_lat, half:half + w] = np.sin(ang)
    cos[n_lat:, :w] = 1.0
    cos[n_lat:, half:half + w] = 1.0
    return jnp.asarray(cos, F32), jnp.asarray(sin, F32)


def _rope_perm(n_freq):
    idx = np.full((LANES,), -1, np.int64)
    f = n_freq
    idx[0:f] = np.arange(0, f)
    idx[f:2 * f] = np.arange(2 * f, 3 * f)
    idx[64:64 + f] = np.arange(f, 2 * f)
    idx[64 + f:64 + 2 * f] = np.arange(3 * f, 4 * f)
    return idx


def _take_cols(w, idx):
    wz = jnp.concatenate([w, jnp.zeros((w.shape[0], 1), w.dtype)], axis=1)
    return wz[:, np.where(idx < 0, w.shape[1], idx)]


def _na_row_cases(rows):
    kh = B_MAX_KH
    i = np.arange(NA_Q_ROWS)[:, None]
    j = np.arange(NA_K_ROWS)[None, :]
    cases = []
    for r0 in (0, NA_Q_ROWS, rows - NA_Q_ROWS):
        start = int(np.clip(r0 - kh // 2, 0, rows - NA_K_ROWS))
        rs = np.clip(r0 + i - kh // 2, 0, rows - kh)
        key_row = start + j
        cases.append(((key_row >= rs) & (key_row < rs + kh), key_row - (r0 + i) + (kh - 1)))
    return np.stack([c[0] for c in cases]), np.stack([c[1] for c in cases])


def _na_bias_kernel(r2_ref, o_ref, *, row_ok, dr_idx):
    n_e = r2_ref.shape[0]
    shape = (GRID_W, LANES)
    c = lax.broadcasted_iota(jnp.int32, shape, 0)
    lane = lax.broadcasted_iota(jnp.int32, shape, 1)
    kc = jnp.where(lane < GRID_W, lane, lane - GRID_W)
    cs = jnp.clip(c - B_KW // 2, 0, GRID_W - B_KW)
    col_ok = (kc >= cs) & (kc < cs + B_KW)
    left = lane < GRID_W
    neg = jnp.full(shape, NEG_BIG, F32)
    x = r2_ref[...]
    tiles = []
    for e in range(n_e):
        t = pltpu.roll(jnp.broadcast_to(x[e:e + 1, :], shape), LANES - (B_KW - 1), 1, stride=1, stride_axis=0)
        tiles.append(jnp.where(col_ok, t * LOG2E, neg))
    for case in range(row_ok.shape[0]):
        for i in range(NA_Q_ROWS):
            for jp in range(NA_K_ROWS // 2):
                ok_l, ok_r = bool(row_ok[case, i, 2 * jp]), bool(row_ok[case, i, 2 * jp + 1])
                e = int(np.clip(dr_idx[case, i, 2 * jp] + 1, 0, n_e - 1))
                if ok_l and ok_r:
                    tile = tiles[e]
                elif ok_l:
                    tile = jnp.where(left, tiles[e], neg)
                elif ok_r:
                    tile = jnp.where(left, neg, tiles[e])
                else:
                    tile = neg
                o_ref[case, i * GRID_W:(i + 1) * GRID_W, jp * LANES:(jp + 1) * LANES] = tile


def _na_bias_tables(rpb, rows):
    n_heads, n_dr, n_dc = rpb.shape
    n_e = n_dr + 1
    r2 = jnp.zeros((n_heads, n_e, LANES), F32)
    r2 = r2.at[:, 1:, :n_dc].set(rpb).at[:, :n_dr, GRID_W:GRID_W + n_dc].set(rpb)
    row_ok, dr_idx = _na_row_cases(rows)
    kern = functools.partial(_na_bias_kernel, row_ok=row_ok, dr_idx=dr_idx)
    tq, tk = NA_Q_ROWS * GRID_W, NA_K_ROWS * GRID_W
    return pl.pallas_call(
        kern,
        grid=(n_heads,),
        in_specs=[pl.BlockSpec((None, n_e, LANES), lambda h: (h, 0, 0))],
        out_specs=pl.BlockSpec((None, 3, tq, tk), lambda h: (h, 0, 0, 0)),
        out_shape=jax.ShapeDtypeStruct((n_heads, 3, tq, tk), F32),
        compiler_params=_cparams(1, 32 * 1024 * 1024),
        name="na_bias",
    )(r2)


def kernel(x, c, ctx, c_ctx, w_ada, b_ada, ln_g, ln_b, w_ffn_gate, w_ffn_up, w_ffn_down, a_w_qkv, a_q_gain, a_k_gain, a_w_o, b_w_qkv, b_rpb, b_w_o, c_w_dqkv, c_q_a_gain, c_kv_a_gain, c_w_uq, c_w_ukv, c_w_o):
    bsz, n_lat, d = x.shape
    n_ctx = ctx.shape[1]
    t = n_lat + n_ctx
    depth = w_ada.shape[0]
    rows = n_lat // GRID_W
    assert n_lat % ROW_TILE == 0 and n_ctx == ROW_TILE and bsz < MOD_ROWS
    assert n_lat % (NA_Q_ROWS * GRID_W) == 0 and rows >= NA_K_ROWS
    assert a_w_qkv.shape[2] == (A_HEADS + 2 * A_KV_HEADS) * A_HEAD_DIM and b_rpb.shape[1] == B_HEADS
    assert c_w_dqkv.shape[2] == C_Q_RANK + C_KV_RANK + C_ROPE and d == B_HEADS * B_HEAD_DIM
    alpha = (2.0 * depth) ** 0.25

    xa = jnp.concatenate([x, ctx], axis=1)
    cond = jnp.zeros((MOD_ROWS, d), F32).at[:bsz].set(c).at[bsz].set(c_ctx)
    mod = _ada_call(cond, w_ada, b_ada)

    cos_a, sin_a = _rope_tables(n_lat, n_ctx, A_HEAD_DIM // 4)
    cos_c, sin_c = _rope_tables(n_lat, n_ctx, C_ROPE // 4)
    perm_a = _rope_perm(A_HEAD_DIM // 4)
    perm_c = _rope_perm(C_ROPE // 4)

    ctx_chunks = ((n_lat, n_ctx),)
    lat_chunks = tuple((s, 1024) for s in range(0, n_lat, 1024)) + ctx_chunks

    for i in range(depth):
        last = i == depth - 1
        out_rows = n_lat if last else t
        kind, j = i % 3, i // 3
        if kind == 0:
            hd = A_HEAD_DIM
            nqk = A_HEADS + A_KV_HEADS
            col_idx = np.concatenate([h * hd + perm_a for h in range(nqk)]
                                     + [np.arange(nqk * hd, (nqk + A_KV_HEADS) * hd)])
            w = a_w_qkv[j][:, col_idx].astype(BF16)
            qkv = _qkv_a_call(xa, mod, i, w, a_q_gain[j][perm_a][None, :], a_k_gain[j][perm_a][None, :],
                              cos_a, sin_a, n_lat)
            o = _flash_call(qkv, qkv, qkv, name="attn_gqa", n_groups=A_KV_HEADS, n_sub=1, n_split=2,
                            n_heads=A_HEADS // A_KV_HEADS, dk=hd, dv=hd, tq=256, k_col0=A_HEADS,
                            v_col0=A_HEADS + A_KV_HEADS, n_lat=n_lat, lat_chunks=lat_chunks,
                            ctx_chunks=ctx_chunks, out_rows=out_rows)
            wo = a_w_o[j]
        elif kind == 1:
            qkv = _qkv_b_call(xa, mod, i, b_w_qkv[j].astype(BF16), n_lat)
            o = _na_call(qkv, _na_bias_tables(b_rpb[j], rows), n_lat, n_ctx, out_rows)
            wo = b_w_o[j]
        else:
            r1 = C_Q_RANK + C_KV_RANK
            wd_idx = np.concatenate([np.arange(r1), np.where(perm_c < 0, -1, r1 + perm_c)])
            wdn = _take_cols(c_w_dqkv[j], wd_idx).astype(BF16)
            hq = C_NOPE + C_ROPE
            uq_idx = np.concatenate([np.concatenate([h * hq + np.arange(C_NOPE),
                                                     np.where(perm_c < 0, -1, h * hq + C_NOPE + perm_c)])
                                     for h in range(C_HEADS)])
            wuq = _take_cols(c_w_uq[j], uq_idx).astype(BF16)
            qc, kc, vc = _proj_c_call(xa, mod, i, wdn, c_q_a_gain[j][None, :], c_kv_a_gain[j][None, :], wuq,
                                      c_w_ukv[j].astype(BF16), cos_c, sin_c, n_lat)
            o = _flash_call(qc, kc, vc, name="attn_mla", n_groups=C_HEADS, n_sub=1, n_split=2, n_heads=1, dk=2 * LANES,
                            dv=C_V, tq=1024, k_col0=0, v_col0=0, n_lat=n_lat, lat_chunks=lat_chunks,
                            ctx_chunks=ctx_chunks, out_rows=out_rows)
            wo = c_w_o[j]
        xa = _post_call(o, xa, mod, i, wo.astype(BF16), w_ffn_gate[i].astype(BF16), w_ffn_up[i].astype(BF16),
                        w_ffn_down[i].astype(BF16), ln_g, ln_b, n_lat, out_rows, alpha)
    return xa
```

```python
import functools
import math

import numpy as np
import jax
import jax.numpy as jnp
from jax import lax
from jax.experimental import pallas as pl
from jax.experimental.pallas import tpu as pltpu

GRID_W = 64
A_HEADS, A_KV_HEADS, A_HEAD_DIM = 8, 2, 128
B_HEADS, B_HEAD_DIM, B_MAX_KH, B_KW = 16, 64, 8, 16
C_HEADS, C_NOPE, C_ROPE, C_V, C_Q_RANK, C_KV_RANK = 8, 128, 64, 128, 384, 256
ROPE_BASE = 10000.0
RMS_EPS = 1e-6
LN_EPS = 1e-5
LOG2E = math.log2(math.e)
NEG_BIG = -1e30

LANES = 128
V7X_VMEM_BYTES = 64 * 1024 * 1024
VMEM_LIMIT = 56 * 1024 * 1024

ROW_TILE = 256
MOD_ROWS = 8
NA_Q_ROWS = 4
NA_K_ROWS = 12
NA_PAIRS_PER_STEP = 4

BF16 = jnp.bfloat16
F32 = jnp.float32


def _cparams(n_axes, vmem=None):
    return pltpu.CompilerParams(dimension_semantics=("arbitrary",) * n_axes, vmem_limit_bytes=vmem)


def _resident(shape, index_map):
    return pl.BlockSpec(shape, index_map, pipeline_mode=pl.Buffered(1))


def _ada_kernel(c_ref, w_ref, b_ref, o_ref):
    c = c_ref[...]
    cs = c / (1.0 + jnp.exp(-c))
    o_ref[...] = jnp.dot(cs, w_ref[...], preferred_element_type=F32) + b_ref[...]


def _ada_call(cond, w_ada, b_ada):
    depth, d, n6 = w_ada.shape
    tn = n6 // 4
    return pl.pallas_call(
        _ada_kernel,
        grid=(depth, n6 // tn),
        in_specs=[
            pl.BlockSpec((MOD_ROWS, d), lambda l, j: (0, 0)),
            pl.BlockSpec((None, d, tn), lambda l, j: (l, 0, j)),
            pl.BlockSpec((None, 1, tn), lambda l, j: (l, 0, j)),
        ],
        out_specs=pl.BlockSpec((None, MOD_ROWS, tn), lambda l, j: (l, 0, j)),
        out_shape=jax.ShapeDtypeStruct((depth, MOD_ROWS, n6), F32),
        compiler_params=_cparams(2, 40 * 1024 * 1024),
        name="ada_mod",
    )(cond, w_ada, b_ada.reshape(depth, 1, n6))


def _mod_row(n_lat_tiles, ctx_row):
    b, i = pl.program_id(0), pl.program_id(1)
    return jnp.where(i < n_lat_tiles, b, ctx_row)


def _mod_vec(mod_ref, row, k, d):
    return mod_ref[pl.ds(row, 1), k * d:(k + 1) * d]


def _modulated(x_ref, mod_ref, row, k_shift, d):
    sh = _mod_vec(mod_ref, row, k_shift, d)
    sc = _mod_vec(mod_ref, row, k_shift + 1, d)
    return (x_ref[...] * (1.0 + sc) + sh).astype(BF16)


def _rms(t, g):
    return t * lax.rsqrt(jnp.mean(t * t, axis=-1, keepdims=True) + RMS_EPS) * g


def _rope(t, cos, sin):
    return t * cos + pltpu.roll(t, LANES // 2, 1) * sin


def _layer_norm(t, g, b):
    mu = jnp.mean(t, axis=-1, keepdims=True)
    c = t - mu
    var = jnp.mean(c * c, axis=-1, keepdims=True)
    return c * lax.rsqrt(var + LN_EPS) * g + b


def _qkv_a_kernel(x_ref, mod_ref, w_ref, gq_ref, gk_ref, cos_ref, sin_ref, o_ref, *, n_lat_tiles, ctx_row, q_scale):
    d = x_ref.shape[-1]
    row = _mod_row(n_lat_tiles, ctx_row)
    h = _modulated(x_ref, mod_ref, row, 0, d)
    acc = jnp.dot(h, w_ref[...], preferred_element_type=F32)
    cos, sin = cos_ref[...], sin_ref[...]
    gq = gq_ref[...] * q_scale
    gk = gk_ref[...]
    hd = A_HEAD_DIM
    for j in range(A_HEADS + A_KV_HEADS):
        t = acc[:, j * hd:(j + 1) * hd]
        y = _rope(_rms(t, gq if j < A_HEADS else gk), cos, sin)
        o_ref[:, j * hd:(j + 1) * hd] = y.astype(BF16)
    v0 = (A_HEADS + A_KV_HEADS) * hd
    o_ref[:, v0:] = acc[:, v0:].astype(BF16)


def _qkv_a_call(xa, mod, layer, w, gq, gk, cos, sin, n_lat):
    bsz, t, d = xa.shape
    n_out = w.shape[1]
    tm = ROW_TILE
    kern = functools.partial(_qkv_a_kernel, n_lat_tiles=n_lat // tm, ctx_row=bsz,
                             q_scale=A_HEAD_DIM ** -0.5 * LOG2E)
    return pl.pallas_call(
        kern,
        grid=(bsz, t // tm),
        in_specs=[
            pl.BlockSpec((None, tm, d), lambda b, i: (b, i, 0)),
            pl.BlockSpec((None, MOD_ROWS, 6 * d), lambda b, i: (layer, 0, 0)),
            _resident((d, n_out), lambda b, i: (0, 0)),
            pl.BlockSpec((1, A_HEAD_DIM), lambda b, i: (0, 0)),
            pl.BlockSpec((1, A_HEAD_DIM), lambda b, i: (0, 0)),
            pl.BlockSpec((tm, LANES), lambda b, i: (i, 0)),
            pl.BlockSpec((tm, LANES), lambda b, i: (i, 0)),
        ],
        out_specs=pl.BlockSpec((None, tm, n_out), lambda b, i: (b, i, 0)),
        out_shape=jax.ShapeDtypeStruct((bsz, t, n_out), BF16),
        compiler_params=_cparams(2, 32 * 1024 * 1024),
        name="qkv_gqa",
    )(xa, mod, w, gq, gk, cos, sin)


def _qkv_b_kernel(x_ref, mod_ref, w_ref, o_ref, *, n_lat_tiles, ctx_row, q_scale):
    d = x_ref.shape[-1]
    row = _mod_row(n_lat_tiles, ctx_row)
    h = _modulated(x_ref, mod_ref, row, 0, d)
    acc = jnp.dot(h, w_ref[...], preferred_element_type=F32)
    o_ref[:, :d] = (acc[:, :d] * q_scale).astype(BF16)
    o_ref[:, d:] = acc[:, d:].astype(BF16)


def _qkv_b_call(xa, mod, layer, w, n_lat):
    bsz, t, d = xa.shape
    n_out = w.shape[1]
    tm = ROW_TILE
    kern = functools.partial(_qkv_b_kernel, n_lat_tiles=n_lat // tm, ctx_row=bsz,
                             q_scale=B_HEAD_DIM ** -0.5 * LOG2E)
    return pl.pallas_call(
        kern,
        grid=(bsz, t // tm),
        in_specs=[
            pl.BlockSpec((None, tm, d), lambda b, i: (b, i, 0)),
            pl.BlockSpec((None, MOD_ROWS, 6 * d), lambda b, i: (layer, 0, 0)),
            _resident((d, n_out), lambda b, i: (0, 0)),
        ],
        out_specs=pl.BlockSpec((None, tm, n_out), lambda b, i: (b, i, 0)),
        out_shape=jax.ShapeDtypeStruct((bsz, t, n_out), BF16),
        compiler_params=_cparams(2, 40 * 1024 * 1024),
        name="qkv_na",
    )(xa, mod, w)


def _proj_c_kernel(x_ref, mod_ref, wd_ref, gq_ref, gkv_ref, wuq_ref, wukv_ref, cos_ref, sin_ref,
                   q_ref, k_ref, v_ref, *, n_lat_tiles, ctx_row, q_scale):
    d = x_ref.shape[-1]
    row = _mod_row(n_lat_tiles, ctx_row)
    h = _modulated(x_ref, mod_ref, row, 0, d)
    cos, sin = cos_ref[...], sin_ref[...]
    dn = jnp.dot(h, wd_ref[...], preferred_element_type=F32)
    r0, r1 = C_Q_RANK, C_Q_RANK + C_KV_RANK
    ql = _rms(dn[:, :r0], gq_ref[...]).astype(BF16)
    kvl = _rms(dn[:, r0:r1], gkv_ref[...]).astype(BF16)
    kr = _rope(dn[:, r1:], cos, sin).astype(BF16)
    q = jnp.dot(ql, wuq_ref[...], preferred_element_type=F32)
    kv = jnp.dot(kvl, wukv_ref[...], preferred_element_type=F32)
    w2 = 2 * LANES
    for hh in range(C_HEADS):
        q_ref[:, hh * w2:hh * w2 + LANES] = (q[:, hh * w2:hh * w2 + LANES] * q_scale).astype(BF16)
        qr = _rope(q[:, hh * w2 + LANES:(hh + 1) * w2], cos, sin) * q_scale
        q_ref[:, hh * w2 + LANES:(hh + 1) * w2] = qr.astype(BF16)
        k_ref[:, hh * w2:hh * w2 + LANES] = kv[:, hh * w2:hh * w2 + LANES].astype(BF16)
        k_ref[:, hh * w2 + LANES:(hh + 1) * w2] = kr
        v_ref[:, hh * LANES:(hh + 1) * LANES] = kv[:, hh * w2 + LANES:(hh + 1) * w2].astype(BF16)


def _proj_c_call(xa, mod, layer, wd, gq, gkv, wuq, wukv, cos, sin, n_lat):
    bsz, t, d = xa.shape
    tm = ROW_TILE
    kern = functools.partial(_proj_c_kernel, n_lat_tiles=n_lat // tm, ctx_row=bsz,
                             q_scale=(C_NOPE + C_ROPE) ** -0.5 * LOG2E)
    const = lambda b, i: (0, 0)
    qk_w = C_HEADS * 2 * LANES
    return pl.pallas_call(
        kern,
        grid=(bsz, t // tm),
        in_specs=[
            pl.BlockSpec((None, tm, d), lambda b, i: (b, i, 0)),
            pl.BlockSpec((None, MOD_ROWS, 6 * d), lambda b, i: (layer, 0, 0)),
            _resident(wd.shape, const),
            pl.BlockSpec(gq.shape, const),
            pl.BlockSpec(gkv.shape, const),
            _resident(wuq.shape, const),
            _resident(wukv.shape, const),
            pl.BlockSpec((tm, LANES), lambda b, i: (i, 0)),
            pl.BlockSpec((tm, LANES), lambda b, i: (i, 0)),
        ],
        out_specs=[
            pl.BlockSpec((None, tm, qk_w), lambda b, i: (b, i, 0)),
            pl.BlockSpec((None, tm, qk_w), lambda b, i: (b, i, 0)),
            pl.BlockSpec((None, tm, C_HEADS * C_V), lambda b, i: (b, i, 0)),
        ],
        out_shape=[
            jax.ShapeDtypeStruct((bsz, t, qk_w), BF16),
            jax.ShapeDtypeStruct((bsz, t, qk_w), BF16),
            jax.ShapeDtypeStruct((bsz, t, C_HEADS * C_V), BF16),
        ],
        compiler_params=_cparams(2, 40 * 1024 * 1024),
        name="proj_mla",
    )(xa, mod, wd, gq, gkv, wuq, wukv, cos, sin)


def _pair_masks(shape):
    lane = lax.broadcasted_iota(jnp.int32, shape, 1)
    return lane < (LANES // 2)


def _dot_nt(a, b):
    return lax.dot_general(a, b, (((1,), (1,)), ((), ())), preferred_element_type=F32)


def _flash_kernel(q_ref, k_ref, v_ref, o_ref, *, n_sub, n_split, n_heads, dk, dv, tq_lat, lat_chunks,
                  ctx_rows, ctx_chunks):
    def run(row0, tq, chunks):
        m_rows = n_heads * tq // n_split
        rows = pl.ds(row0, tq)
        chains, qs, state = [], [], []
        for u in range(n_sub):
            q0 = u * n_heads * dk
            q = jnp.concatenate([q_ref[rows, q0 + r * dk:q0 + (r + 1) * dk] for r in range(n_heads)], axis=0)
            for h in range(n_split):
                chains.append(u)
                qs.append(q[h * m_rows:(h + 1) * m_rows])
                state.append((jnp.full((m_rows, 1), NEG_BIG, F32), jnp.zeros((m_rows, dv + LANES), F32)))
        for start, size in chunks:
            ones = jnp.ones((size, LANES), BF16)
            scores = [_dot_nt(qs[c], k_ref[start:start + size, u * dk:(u + 1) * dk])
                      for c, u in enumerate(chains)]
            for c, u in enumerate(chains):
                m, acc = state[c]
                v1 = jnp.concatenate([v_ref[start:start + size, u * dv:(u + 1) * dv], ones], axis=1)
                m_new = jnp.maximum(m, jnp.max(scores[c], axis=-1, keepdims=True))
                alpha = jnp.exp2(m - m_new)
                p = jnp.exp2(scores[c] - m_new)
                acc = alpha * acc + jnp.dot(p.astype(BF16), v1, preferred_element_type=F32)
                state[c] = (m_new, acc)
        for u in range(n_sub):
            accs = [state[u * n_split + h][1] for h in range(n_split)]
            acc = accs[0] if n_split == 1 else jnp.concatenate(accs, axis=0)
            o = acc[:, :dv] / acc[:, dv:]
            o0 = u * n_heads * dv
            for r in range(n_heads):
                o_ref[rows, o0 + r * dv:o0 + (r + 1) * dv] = o[r * tq:(r + 1) * tq].astype(o_ref.dtype)

    n_ctx_steps = 0 if ctx_rows is None else 1
    i = pl.program_id(2) - n_ctx_steps

    @pl.when(i >= 0)
    def _():
        run(pl.multiple_of(i * tq_lat, tq_lat), tq_lat, lat_chunks)

    if ctx_rows is not None:
        @pl.when(i < 0)
        def _():
            run(ctx_rows[0], ctx_rows[1], ctx_chunks)


def _flash_call(q_arr, k_arr, v_arr, *, name, n_groups, n_sub, n_split, n_heads, dk, dv, tq, k_col0, v_col0, n_lat,
                lat_chunks, ctx_chunks, out_rows):
    bsz, t, _ = k_arr.shape
    assert n_groups % n_sub == 0 and k_col0 % n_sub == 0 and v_col0 % n_sub == 0 and n_lat % tq == 0
    with_ctx = out_rows > n_lat
    kern = functools.partial(_flash_kernel, n_sub=n_sub, n_split=n_split, n_heads=n_heads, dk=dk, dv=dv, tq_lat=tq,
                             lat_chunks=lat_chunks,
                             ctx_rows=(n_lat, out_rows - n_lat) if with_ctx else None, ctx_chunks=ctx_chunks)
    return pl.pallas_call(
        kern,
        grid=(bsz, n_groups // n_sub, n_lat // tq + int(with_ctx)),
        in_specs=[
            pl.BlockSpec((None, t, n_sub * n_heads * dk), lambda b, g, i: (b, 0, g)),
            pl.BlockSpec((None, t, n_sub * dk), lambda b, g, i: (b, 0, k_col0 // n_sub + g)),
            pl.BlockSpec((None, t, n_sub * dv), lambda b, g, i: (b, 0, v_col0 // n_sub + g)),
        ],
        out_specs=pl.BlockSpec((None, out_rows, n_sub * n_heads * dv), lambda b, g, i: (b, 0, g)),
        out_shape=jax.ShapeDtypeStruct((bsz, out_rows, n_groups * n_heads * dv), BF16),
        compiler_params=_cparams(3, VMEM_LIMIT),
        name=name,
    )(q_arr, k_arr, v_arr)


def _na_kernel(q_ref, k_ref, v_ref, bias_ref, o_ref, *, n_lat, n_ctx, n_groups, n_ctx_steps):
    gi = pl.program_id(2) - n_ctx_steps
    rows = n_groups * NA_Q_ROWS
    tq = NA_Q_ROWS * GRID_W
    tk = NA_K_ROWS * GRID_W
    lo = _pair_masks((tq, LANES))

    def stacked_q(cols):
        qb = q_ref[:, cols]
        zero = jnp.zeros_like(qb)
        return jnp.concatenate([jnp.where(lo, qb, zero), jnp.where(lo, zero, qb)], axis=0)

    def with_ones(v):
        return jnp.concatenate([v, jnp.ones(v.shape, v.dtype)], axis=1)

    def finish(o1, cols):
        o = o1[:, :LANES] / o1[:, LANES:]
        o_ref[:, cols] = jnp.where(lo, o[:tq], o[tq:]).astype(o_ref.dtype)

    @pl.when(gi >= 0)
    def _():
        start = jnp.clip(gi * NA_Q_ROWS - B_MAX_KH // 2, 0, rows - NA_K_ROWS) * GRID_W
        start = pl.multiple_of(start, GRID_W)
        scores = []
        for s in range(NA_PAIRS_PER_STEP):
            cols = slice(s * LANES, (s + 1) * LANES)
            kk = jnp.concatenate([k_ref[pl.ds(start, tk), cols], k_ref[n_lat:n_lat + n_ctx, cols]], axis=0)
            scores.append(_dot_nt(stacked_q(cols), kk))
        for s in range(NA_PAIRS_PER_STEP):
            cols = slice(s * LANES, (s + 1) * LANES)
            vv = with_ones(jnp.concatenate([v_ref[pl.ds(start, tk), cols], v_ref[n_lat:n_lat + n_ctx, cols]], axis=0))
            bias = bias_ref[2 * s:2 * s + 2].reshape(2 * tq, tk)
            sc = jnp.concatenate([scores[s][:, :tk] + bias, scores[s][:, tk:]], axis=1)
            p = jnp.exp2(sc - jnp.max(sc, axis=-1, keepdims=True))
            finish(jnp.dot(p.astype(BF16), vv, preferred_element_type=F32), cols)

    @pl.when(gi < 0)
    def _():
        for s in range(NA_PAIRS_PER_STEP):
            cols = slice(s * LANES, (s + 1) * LANES)
            q = stacked_q(cols)
            s_c = _dot_nt(q, k_ref[n_lat:n_lat + n_ctx, cols])
            p_c = jnp.exp2(s_c - jnp.max(s_c, axis=-1, keepdims=True))
            vc = with_ones(v_ref[n_lat:n_lat + n_ctx, cols])
            finish(jnp.dot(p_c.astype(BF16), vc, preferred_element_type=F32), cols)


def _na_call(qkv, bias, n_lat, n_ctx, out_rows):
    bsz, t, w3 = qkv.shape
    d = w3 // 3
    w = NA_PAIRS_PER_STEP * LANES
    n_steps = d // w
    tq = NA_Q_ROWS * GRID_W
    tk = NA_K_ROWS * GRID_W
    n_groups = n_lat // tq
    n_ctx_steps = out_rows // tq - n_groups
    kern = functools.partial(_na_kernel, n_lat=n_lat, n_ctx=n_ctx, n_groups=n_groups, n_ctx_steps=n_ctx_steps)

    def row_tile(g):
        return jnp.where(g < n_ctx_steps, n_groups, g - n_ctx_steps)

    def bias_map(b, p, g):
        gi = g - n_ctx_steps
        case = jnp.where(gi <= 0, 0, jnp.where(gi == n_groups - 1, 2, 1))
        return (p, case, 0, 0)

    assert n_ctx == tq and n_ctx_steps in (0, 1)
    return pl.pallas_call(
        kern,
        grid=(bsz, n_steps, out_rows // tq),
        in_specs=[
            pl.BlockSpec((None, tq, w), lambda b, p, g: (b, row_tile(g), p)),
            pl.BlockSpec((None, t, w), lambda b, p, g: (b, 0, n_steps + p)),
            pl.BlockSpec((None, t, w), lambda b, p, g: (b, 0, 2 * n_steps + p)),
            pl.BlockSpec((2 * NA_PAIRS_PER_STEP, None, tq, tk), bias_map),
        ],
        out_specs=pl.BlockSpec((None, tq, w), lambda b, p, g: (b, row_tile(g), p)),
        out_shape=jax.ShapeDtypeStruct((bsz, out_rows, d), BF16),
        compiler_params=_cparams(3, 48 * 1024 * 1024),
        name="attn_na",
    )(qkv, qkv, qkv, bias)


def _post_kernel(o_ref, x_ref, mod_ref, wo_ref, wg_ref, wu_ref, wd_ref, lng_ref, lnb_ref, out_ref,
                 *, n_lat_tiles, ctx_row, alpha):
    d = x_ref.shape[-1]
    row = _mod_row(n_lat_tiles, ctx_row)
    g1 = _mod_vec(mod_ref, row, 2, d)
    sh2 = _mod_vec(mod_ref, row, 3, d)
    sc2 = _mod_vec(mod_ref, row, 4, d)
    g2 = _mod_vec(mod_ref, row, 5, d)
    y = jnp.dot(o_ref[...], wo_ref[...], preferred_element_type=F32)
    x1 = _layer_norm(alpha * x_ref[...] + g1 * y, lng_ref[0:1, :], lnb_ref[0:1, :])
    h = (x1 * (1.0 + sc2) + sh2).astype(BF16)
    gate = jnp.dot(h, wg_ref[...], preferred_element_type=F32)
    up = jnp.dot(h, wu_ref[...], preferred_element_type=F32)
    a = (gate / (1.0 + jnp.exp(-gate)) * up).astype(BF16)
    f = jnp.dot(a, wd_ref[...], preferred_element_type=F32)
    out_ref[...] = _layer_norm(alpha * x1 + g2 * f, lng_ref[1:2, :], lnb_ref[1:2, :])


def _post_call(o, xa, mod, layer, wo_all, wo_idx, wg_all, wu_all, wd_all, lng, lnb, n_lat, out_rows, alpha):
    bsz, _, d = xa.shape
    tm = ROW_TILE
    d_ff = wg_all.shape[2]
    kern = functools.partial(_post_kernel, n_lat_tiles=n_lat // tm, ctx_row=bsz, alpha=alpha)
    return pl.pallas_call(
        kern,
        grid=(bsz, out_rows // tm),
        in_specs=[
            pl.BlockSpec((None, tm, o.shape[-1]), lambda b, i: (b, i, 0)),
            pl.BlockSpec((None, tm, d), lambda b, i: (b, i, 0)),
            pl.BlockSpec((None, MOD_ROWS, 6 * d), lambda b, i: (layer, 0, 0)),
            _resident((None,) + wo_all.shape[1:], lambda b, i: (wo_idx, 0, 0)),
            _resident((None, d, d_ff), lambda b, i: (layer, 0, 0)),
            _resident((None, d, d_ff), lambda b, i: (layer, 0, 0)),
            _resident((None, d_ff, d), lambda b, i: (layer, 0, 0)),
            pl.BlockSpec((None, 2, d), lambda b, i: (layer, 0, 0)),
            pl.BlockSpec((None, 2, d), lambda b, i: (layer, 0, 0)),
        ],
        out_specs=pl.BlockSpec((None, tm, d), lambda b, i: (b, i, 0)),
        out_shape=jax.ShapeDtypeStruct((bsz, out_rows, d), F32),
        compiler_params=_cparams(2, VMEM_LIMIT),
        name="post_ffn",
    )(o, xa, mod, wo_all, wg_all, wu_all, wd_all, lng, lnb)


def _rope_tables(n_lat, n_ctx, n_freq):
    half = LANES // 2
    t = np.arange(n_lat)
    freqs = ROPE_BASE ** (-np.arange(0, 2 * n_freq, 2, dtype=np.float64) / (2 * n_freq))
    ang = np.concatenate([(t // GRID_W)[:, None] * freqs[None, :], (t % GRID_W)[:, None] * freqs[None, :]], axis=1)
    cos = np.zeros((n_lat + n_ctx, LANES))
    sin = np.zeros((n_lat + n_ctx, LANES))
    w = 2 * n_freq
    cos[:n_lat, :w] = np.cos(ang)
    cos[:n_lat, half:half + w] = np.cos(ang)
    sin[:n_lat, :w] = -np.sin(ang)
    sin[:n_lat, half:half + w] = np.sin(ang)
    cos[n_lat:, :w] = 1.0
    cos[n_lat:, half:half + w] = 1.0
    return jnp.asarray(cos, F32), jnp.asarray(sin, F32)


def _rope_perm(n_freq):
    idx = np.full((LANES,), -1, np.int64)
    f = n_freq
    idx[0:f] = np.arange(0, f)
    idx[f:2 * f] = np.arange(2 * f, 3 * f)
    idx[64:64 + f] = np.arange(f, 2 * f)
    idx[64 + f:64 + 2 * f] = np.arange(3 * f, 4 * f)
    return idx


def _take_cols(w, idx):
    wz = jnp.concatenate([w, jnp.zeros((w.shape[0], 1), w.dtype)], axis=1)
    return wz[:, np.where(idx < 0, w.shape[1], idx)]


def _na_row_cases(rows):
    kh = B_MAX_KH
    i = np.arange(NA_Q_ROWS)[:, None]
    j = np.arange(NA_K_ROWS)[None, :]
    cases = []
    for r0 in (0, NA_Q_ROWS, rows - NA_Q_ROWS):
        start = int(np.clip(r0 - kh // 2, 0, rows - NA_K_ROWS))
        rs = np.clip(r0 + i - kh // 2, 0, rows - kh)
        key_row = start + j
        cases.append(((key_row >= rs) & (key_row < rs + kh), key_row - (r0 + i) + (kh - 1)))
    return np.stack([c[0] for c in cases]), np.stack([c[1] for c in cases])


def _na_bias_kernel(r2_ref, o_ref, *, row_ok, dr_idx):
    n_e = r2_ref.shape[0]
    shape = (GRID_W, LANES)
    c = lax.broadcasted_iota(jnp.int32, shape, 0)
    lane = lax.broadcasted_iota(jnp.int32, shape, 1)
    kc = jnp.where(lane < GRID_W, lane, lane - GRID_W)
    cs = jnp.clip(c - B_KW // 2, 0, GRID_W - B_KW)
    col_ok = (kc >= cs) & (kc < cs + B_KW)
    left = lane < GRID_W
    neg = jnp.full(shape, NEG_BIG, F32)
    x = r2_ref[...]
    tiles = []
    for e in range(n_e):
        t = pltpu.roll(jnp.broadcast_to(x[e:e + 1, :], shape), LANES - (B_KW - 1), 1, stride=1, stride_axis=0)
        tiles.append(jnp.where(col_ok, t * LOG2E, neg))
    for case in range(row_ok.shape[0]):
        for i in range(NA_Q_ROWS):
            for jp in range(NA_K_ROWS // 2):
                ok_l, ok_r = bool(row_ok[case, i, 2 * jp]), bool(row_ok[case, i, 2 * jp + 1])
                e = int(np.clip(dr_idx[case, i, 2 * jp] + 1, 0, n_e - 1))
                if ok_l and ok_r:
                    tile = tiles[e]
                elif ok_l:
                    tile = jnp.where(left, tiles[e], neg)
                elif ok_r:
                    tile = jnp.where(left, neg, tiles[e])
                else:
                    tile = neg
                o_ref[case, i * GRID_W:(i + 1) * GRID_W, jp * LANES:(jp + 1) * LANES] = tile


def _na_bias_tables(rpb, rows):
    n_heads, n_dr, n_dc = rpb.shape
    n_e = n_dr + 1
    r2 = jnp.zeros((n_heads, n_e, LANES), F32)
    r2 = r2.at[:, 1:, :n_dc].set(rpb).at[:, :n_dr, GRID_W:GRID_W + n_dc].set(rpb)
    row_ok, dr_idx = _na_row_cases(rows)
    kern = functools.partial(_na_bias_kernel, row_ok=row_ok, dr_idx=dr_idx)
    tq, tk = NA_Q_ROWS * GRID_W, NA_K_ROWS * GRID_W
    return pl.pallas_call(
        kern,
        grid=(n_heads,),
        in_specs=[pl.BlockSpec((None, n_e, LANES), lambda h: (h, 0, 0))],
        out_specs=pl.BlockSpec((None, 3, tq, tk), lambda h: (h, 0, 0, 0)),
        out_shape=jax.ShapeDtypeStruct((n_heads, 3, tq, tk), F32),
        compiler_params=_cparams(1, 32 * 1024 * 1024),
        name="na_bias",
    )(r2)


def kernel(x, c, ctx, c_ctx, w_ada, b_ada, ln_g, ln_b, w_ffn_gate, w_ffn_up, w_ffn_down, a_w_qkv, a_q_gain, a_k_gain, a_w_o, b_w_qkv, b_rpb, b_w_o, c_w_dqkv, c_q_a_gain, c_kv_a_gain, c_w_uq, c_w_ukv, c_w_o):
    bsz, n_lat, d = x.shape
    n_ctx = ctx.shape[1]
    t = n_lat + n_ctx
    depth = w_ada.shape[0]
    rows = n_lat // GRID_W
    assert n_lat % ROW_TILE == 0 and n_ctx == ROW_TILE and bsz < MOD_ROWS
    assert n_lat % (NA_Q_ROWS * GRID_W) == 0 and rows >= NA_K_ROWS
    assert a_w_qkv.shape[2] == (A_HEADS + 2 * A_KV_HEADS) * A_HEAD_DIM and b_rpb.shape[1] == B_HEADS
    assert c_w_dqkv.shape[2] == C_Q_RANK + C_KV_RANK + C_ROPE and d == B_HEADS * B_HEAD_DIM
    alpha = (2.0 * depth) ** 0.25

    xa = jnp.concatenate([x, ctx], axis=1)
    cond = jnp.zeros((MOD_ROWS, d), F32).at[:bsz].set(c).at[bsz].set(c_ctx)
    mod = _ada_call(cond, w_ada, b_ada)

    cos_a, sin_a = _rope_tables(n_lat, n_ctx, A_HEAD_DIM // 4)
    cos_c, sin_c = _rope_tables(n_lat, n_ctx, C_ROPE // 4)
    perm_a = _rope_perm(A_HEAD_DIM // 4)
    perm_c = _rope_perm(C_ROPE // 4)

    wg_all, wu_all, wd_all = w_ffn_gate.astype(BF16), w_ffn_up.astype(BF16), w_ffn_down.astype(BF16)
    wo_a, wo_b, wo_c = a_w_o.astype(BF16), b_w_o.astype(BF16), c_w_o.astype(BF16)

    ctx_chunks = ((n_lat, n_ctx),)
    lat_chunks = tuple((s, 1024) for s in range(0, n_lat, 1024)) + ctx_chunks

    for i in range(depth):
        last = i == depth - 1
        out_rows = n_lat if last else t
        kind, j = i % 3, i // 3
        if kind == 0:
            hd = A_HEAD_DIM
            nqk = A_HEADS + A_KV_HEADS
            col_idx = np.concatenate([h * hd + perm_a for h in range(nqk)]
                                     + [np.arange(nqk * hd, (nqk + A_KV_HEADS) * hd)])
            w = a_w_qkv[j][:, col_idx].astype(BF16)
            qkv = _qkv_a_call(xa, mod, i, w, a_q_gain[j][perm_a][None, :], a_k_gain[j][perm_a][None, :],
                              cos_a, sin_a, n_lat)
            o = _flash_call(qkv, qkv, qkv, name="attn_gqa", n_groups=A_KV_HEADS, n_sub=1, n_split=2,
                            n_heads=A_HEADS // A_KV_HEADS, dk=hd, dv=hd, tq=256, k_col0=A_HEADS,
                            v_col0=A_HEADS + A_KV_HEADS, n_lat=n_lat, lat_chunks=lat_chunks,
                            ctx_chunks=ctx_chunks, out_rows=out_rows)
            wo_all = wo_a
        elif kind == 1:
            qkv = _qkv_b_call(xa, mod, i, b_w_qkv[j].astype(BF16), n_lat)
            o = _na_call(qkv, _na_bias_tables(b_rpb[j], rows), n_lat, n_ctx, out_rows)
            wo_all = wo_b
        else:
            r1 = C_Q_RANK + C_KV_RANK
            wd_idx = np.concatenate([np.arange(r1), np.where(perm_c < 0, -1, r1 + perm_c)])
            wdn = _take_cols(c_w_dqkv[j], wd_idx).astype(BF16)
            hq = C_NOPE + C_ROPE
            uq_idx = np.concatenate([np.concatenate([h * hq + np.arange(C_NOPE),
                                                     np.where(perm_c < 0, -1, h * hq + C_NOPE + perm_c)])
                                     for h in range(C_HEADS)])
            wuq = _take_cols(c_w_uq[j], uq_idx).astype(BF16)
            qc, kc, vc = _proj_c_call(xa, mod, i, wdn, c_q_a_gain[j][None, :], c_kv_a_gain[j][None, :], wuq,
                                      c_w_ukv[j].astype(BF16), cos_c, sin_c, n_lat)
            o = _flash_call(qc, kc, vc, name="attn_mla", n_groups=C_HEADS, n_sub=1, n_split=2, n_heads=1, dk=2 * LANES,
                            dv=C_V, tq=1024, k_col0=0, v_col0=0, n_lat=n_lat, lat_chunks=lat_chunks,
                            ctx_chunks=ctx_chunks, out_rows=out_rows)
            wo_all = wo_c
        xa = _post_call(o, xa, mod, i, wo_all, j, wg_all, wu_all, wd_all, ln_g, ln_b, n_lat, out_rows, alpha)
    return xa
```

```python
import functools
import math

import numpy as np
import jax
import jax.numpy as jnp
from jax import lax
from jax.experimental import pallas as pl
from jax.experimental.pallas import tpu as pltpu

GRID_W = 64
A_HEADS, A_KV_HEADS, A_HEAD_DIM = 8, 2, 128
B_HEADS, B_HEAD_DIM, B_MAX_KH, B_KW = 16, 64, 8, 16
C_HEADS, C_NOPE, C_ROPE, C_V, C_Q_RANK, C_KV_RANK = 8, 128, 64, 128, 384, 256
ROPE_BASE = 10000.0
RMS_EPS = 1e-6
LN_EPS = 1e-5
LOG2E = math.log2(math.e)
NEG_BIG = -1e30

LANES = 128
V7X_VMEM_BYTES = 64 * 1024 * 1024
VMEM_LIMIT = 56 * 1024 * 1024

ROW_TILE = 256
ROW_BATCHES = 2
MOD_ROWS = 8
NA_Q_ROWS = 4
NA_K_ROWS = 12
NA_PAIRS_PER_STEP = 4

BF16 = jnp.bfloat16
F32 = jnp.float32


def _cparams(n_axes, vmem=None):
    return pltpu.CompilerParams(dimension_semantics=("arbitrary",) * n_axes, vmem_limit_bytes=vmem)


def _resident(shape, index_map):
    return pl.BlockSpec(shape, index_map, pipeline_mode=pl.Buffered(1))


def _ada_kernel(c_ref, w_ref, b_ref, o_ref):
    c = c_ref[...]
    cs = c / (1.0 + jnp.exp(-c))
    o_ref[...] = jnp.dot(cs, w_ref[...], preferred_element_type=F32) + b_ref[...]


def _ada_call(cond, w_ada, b_ada):
    depth, d, n6 = w_ada.shape
    tn = n6 // 4
    return pl.pallas_call(
        _ada_kernel,
        grid=(depth, n6 // tn),
        in_specs=[
            pl.BlockSpec((MOD_ROWS, d), lambda l, j: (0, 0)),
            pl.BlockSpec((None, d, tn), lambda l, j: (l, 0, j)),
            pl.BlockSpec((None, 1, tn), lambda l, j: (l, 0, j)),
        ],
        out_specs=pl.BlockSpec((None, MOD_ROWS, tn), lambda l, j: (l, 0, j)),
        out_shape=jax.ShapeDtypeStruct((depth, MOD_ROWS, n6), F32),
        compiler_params=_cparams(2, 40 * 1024 * 1024),
        name="ada_mod",
    )(cond, w_ada, b_ada.reshape(depth, 1, n6))


def _mod_rows(nb, n_lat_tiles, ctx_row):
    b, i = pl.program_id(0), pl.program_id(1)
    return [jnp.where(i < n_lat_tiles, b * nb + s, ctx_row) for s in range(nb)]


def _mod_vec(mod_ref, row, k, d):
    return mod_ref[pl.ds(row, 1), k * d:(k + 1) * d]


def _modulated(x_ref, mod_ref, rows, k_shift, d):
    parts = []
    for s, row in enumerate(rows):
        sh = _mod_vec(mod_ref, row, k_shift, d)
        sc = _mod_vec(mod_ref, row, k_shift + 1, d)
        parts.append((x_ref[s] * (1.0 + sc) + sh).astype(BF16))
    return jnp.concatenate(parts, axis=0)


def _store_rows(o_ref, cols, val):
    nb, tm, _ = o_ref.shape
    o_ref[:, :, cols] = val.astype(o_ref.dtype).reshape(nb, tm, val.shape[-1])


def _row_block(nb, tm, w):
    return pl.BlockSpec((nb, tm, w), lambda b, i: (b, i, 0))


def _table_rows(ref, nb):
    t = ref[...]
    return t if nb == 1 else jnp.concatenate([t] * nb, axis=0)


def _rms(t, g):
    return t * lax.rsqrt(jnp.mean(t * t, axis=-1, keepdims=True) + RMS_EPS) * g


def _rope(t, cos, sin):
    return t * cos + pltpu.roll(t, LANES // 2, 1) * sin


def _layer_norm(t, g, b):
    mu = jnp.mean(t, axis=-1, keepdims=True)
    c = t - mu
    var = jnp.mean(c * c, axis=-1, keepdims=True)
    return c * lax.rsqrt(var + LN_EPS) * g + b


def _qkv_a_kernel(x_ref, mod_ref, w_ref, gq_ref, gk_ref, cos_ref, sin_ref, o_ref, *, n_lat_tiles, ctx_row, q_scale):
    nb, _, d = x_ref.shape
    h = _modulated(x_ref, mod_ref, _mod_rows(nb, n_lat_tiles, ctx_row), 0, d)
    acc = jnp.dot(h, w_ref[...], preferred_element_type=F32)
    cos, sin = _table_rows(cos_ref, nb), _table_rows(sin_ref, nb)
    gq = gq_ref[...] * q_scale
    gk = gk_ref[...]
    hd = A_HEAD_DIM
    for j in range(A_HEADS + A_KV_HEADS):
        t = acc[:, j * hd:(j + 1) * hd]
        y = _rope(_rms(t, gq if j < A_HEADS else gk), cos, sin)
        _store_rows(o_ref, slice(j * hd, (j + 1) * hd), y)
    v0 = (A_HEADS + A_KV_HEADS) * hd
    _store_rows(o_ref, slice(v0, acc.shape[1]), acc[:, v0:])


def _qkv_a_call(xa, mod, layer, w, gq, gk, cos, sin, n_lat):
    bsz, t, d = xa.shape
    n_out = w.shape[1]
    tm = ROW_TILE
    kern = functools.partial(_qkv_a_kernel, n_lat_tiles=n_lat // tm, ctx_row=bsz,
                             q_scale=A_HEAD_DIM ** -0.5 * LOG2E)
    nb = 1
    return pl.pallas_call(
        kern,
        grid=(bsz // nb, t // tm),
        in_specs=[
            _row_block(nb, tm, d),
            pl.BlockSpec((None, MOD_ROWS, 6 * d), lambda b, i: (layer, 0, 0)),
            _resident((d, n_out), lambda b, i: (0, 0)),
            pl.BlockSpec((1, A_HEAD_DIM), lambda b, i: (0, 0)),
            pl.BlockSpec((1, A_HEAD_DIM), lambda b, i: (0, 0)),
            pl.BlockSpec((tm, LANES), lambda b, i: (i, 0)),
            pl.BlockSpec((tm, LANES), lambda b, i: (i, 0)),
        ],
        out_specs=_row_block(nb, tm, n_out),
        out_shape=jax.ShapeDtypeStruct((bsz, t, n_out), BF16),
        compiler_params=_cparams(2, 32 * 1024 * 1024),
        name="qkv_gqa",
    )(xa, mod, w, gq, gk, cos, sin)


def _qkv_b_kernel(x_ref, mod_ref, w_ref, o_ref, *, n_lat_tiles, ctx_row, q_scale):
    nb, _, d = x_ref.shape
    h = _modulated(x_ref, mod_ref, _mod_rows(nb, n_lat_tiles, ctx_row), 0, d)
    acc = jnp.dot(h, w_ref[...], preferred_element_type=F32)
    _store_rows(o_ref, slice(0, d), acc[:, :d] * q_scale)
    _store_rows(o_ref, slice(d, acc.shape[1]), acc[:, d:])


def _qkv_b_call(xa, mod, layer, w, n_lat):
    bsz, t, d = xa.shape
    n_out = w.shape[1]
    tm = ROW_TILE
    kern = functools.partial(_qkv_b_kernel, n_lat_tiles=n_lat // tm, ctx_row=bsz,
                             q_scale=B_HEAD_DIM ** -0.5 * LOG2E)
    return pl.pallas_call(
        kern,
        grid=(bsz // ROW_BATCHES, t // tm),
        in_specs=[
            _row_block(ROW_BATCHES, tm, d),
            pl.BlockSpec((None, MOD_ROWS, 6 * d), lambda b, i: (layer, 0, 0)),
            _resident((d, n_out), lambda b, i: (0, 0)),
        ],
        out_specs=_row_block(ROW_BATCHES, tm, n_out),
        out_shape=jax.ShapeDtypeStruct((bsz, t, n_out), BF16),
        compiler_params=_cparams(2, 40 * 1024 * 1024),
        name="qkv_na",
    )(xa, mod, w)


def _proj_c_kernel(x_ref, mod_ref, wd_ref, gq_ref, gkv_ref, wuq_ref, wukv_ref, cos_ref, sin_ref,
                   q_ref, k_ref, v_ref, *, n_lat_tiles, ctx_row, q_scale):
    nb, _, d = x_ref.shape
    h = _modulated(x_ref, mod_ref, _mod_rows(nb, n_lat_tiles, ctx_row), 0, d)
    cos, sin = _table_rows(cos_ref, nb), _table_rows(sin_ref, nb)
    dn = jnp.dot(h, wd_ref[...], preferred_element_type=F32)
    r0, r1 = C_Q_RANK, C_Q_RANK + C_KV_RANK
    ql = _rms(dn[:, :r0], gq_ref[...]).astype(BF16)
    kvl = _rms(dn[:, r0:r1], gkv_ref[...]).astype(BF16)
    kr = _rope(dn[:, r1:], cos, sin)
    q = jnp.dot(ql, wuq_ref[...], preferred_element_type=F32)
    kv = jnp.dot(kvl, wukv_ref[...], preferred_element_type=F32)
    w2 = 2 * LANES
    for hh in range(C_HEADS):
        nope, rot = slice(hh * w2, hh * w2 + LANES), slice(hh * w2 + LANES, (hh + 1) * w2)
        _store_rows(q_ref, nope, q[:, nope] * q_scale)
        _store_rows(q_ref, rot, _rope(q[:, rot], cos, sin) * q_scale)
        _store_rows(k_ref, nope, kv[:, nope])
        _store_rows(k_ref, rot, kr)
        _store_rows(v_ref, slice(hh * LANES, (hh + 1) * LANES), kv[:, rot])


def _proj_c_call(xa, mod, layer, wd, gq, gkv, wuq, wukv, cos, sin, n_lat):
    bsz, t, d = xa.shape
    tm = ROW_TILE
    kern = functools.partial(_proj_c_kernel, n_lat_tiles=n_lat // tm, ctx_row=bsz,
                             q_scale=(C_NOPE + C_ROPE) ** -0.5 * LOG2E)
    const = lambda b, i: (0, 0)
    qk_w = C_HEADS * 2 * LANES
    return pl.pallas_call(
        kern,
        grid=(bsz // ROW_BATCHES, t // tm),
        in_specs=[
            _row_block(ROW_BATCHES, tm, d),
            pl.BlockSpec((None, MOD_ROWS, 6 * d), lambda b, i: (layer, 0, 0)),
            _resident(wd.shape, const),
            pl.BlockSpec(gq.shape, const),
            pl.BlockSpec(gkv.shape, const),
            _resident(wuq.shape, const),
            _resident(wukv.shape, const),
            pl.BlockSpec((tm, LANES), lambda b, i: (i, 0)),
            pl.BlockSpec((tm, LANES), lambda b, i: (i, 0)),
        ],
        out_specs=[_row_block(ROW_BATCHES, tm, qk_w), _row_block(ROW_BATCHES, tm, qk_w),
                   _row_block(ROW_BATCHES, tm, C_HEADS * C_V)],
        out_shape=[
            jax.ShapeDtypeStruct((bsz, t, qk_w), BF16),
            jax.ShapeDtypeStruct((bsz, t, qk_w), BF16),
            jax.ShapeDtypeStruct((bsz, t, C_HEADS * C_V), BF16),
        ],
        compiler_params=_cparams(2, 40 * 1024 * 1024),
        name="proj_mla",
    )(xa, mod, wd, gq, gkv, wuq, wukv, cos, sin)


def _pair_masks(shape):
    lane = lax.broadcasted_iota(jnp.int32, shape, 1)
    return lane < (LANES // 2)


def _dot_nt(a, b):
    return lax.dot_general(a, b, (((1,), (1,)), ((), ())), preferred_element_type=F32)


def _flash_kernel(q_ref, k_ref, v_ref, o_ref, *, n_sub, n_split, n_heads, dk, dv, tq_lat, lat_chunks,
                  ctx_rows, ctx_chunks):
    def run(row0, tq, chunks):
        m_rows = n_heads * tq // n_split
        rows = pl.ds(row0, tq)
        chains, qs, state = [], [], []
        for u in range(n_sub):
            q0 = u * n_heads * dk
            q = jnp.concatenate([q_ref[rows, q0 + r * dk:q0 + (r + 1) * dk] for r in range(n_heads)], axis=0)
            for h in range(n_split):
                chains.append(u)
                qs.append(q[h * m_rows:(h + 1) * m_rows])
                state.append((jnp.full((m_rows, 1), NEG_BIG, F32), jnp.zeros((m_rows, dv + LANES), F32)))
        for start, size in chunks:
            ones = jnp.ones((size, LANES), BF16)
            scores = [_dot_nt(qs[c], k_ref[start:start + size, u * dk:(u + 1) * dk])
                      for c, u in enumerate(chains)]
            for c, u in enumerate(chains):
                m, acc = state[c]
                v1 = jnp.concatenate([v_ref[start:start + size, u * dv:(u + 1) * dv], ones], axis=1)
                m_new = jnp.maximum(m, jnp.max(scores[c], axis=-1, keepdims=True))
                alpha = jnp.exp2(m - m_new)
                p = jnp.exp2(scores[c] - m_new)
                acc = alpha * acc + jnp.dot(p.astype(BF16), v1, preferred_element_type=F32)
                state[c] = (m_new, acc)
        for u in range(n_sub):
            accs = [state[u * n_split + h][1] for h in range(n_split)]
            acc = accs[0] if n_split == 1 else jnp.concatenate(accs, axis=0)
            o = acc[:, :dv] / acc[:, dv:]
            o0 = u * n_heads * dv
            for r in range(n_heads):
                o_ref[rows, o0 + r * dv:o0 + (r + 1) * dv] = o[r * tq:(r + 1) * tq].astype(o_ref.dtype)

    n_ctx_steps = 0 if ctx_rows is None else 1
    i = pl.program_id(2) - n_ctx_steps

    @pl.when(i >= 0)
    def _():
        run(pl.multiple_of(i * tq_lat, tq_lat), tq_lat, lat_chunks)

    if ctx_rows is not None:
        @pl.when(i < 0)
        def _():
            run(ctx_rows[0], ctx_rows[1], ctx_chunks)


def _flash_call(q_arr, k_arr, v_arr, *, name, n_groups, n_sub, n_split, n_heads, dk, dv, tq, k_col0, v_col0, n_lat,
                lat_chunks, ctx_chunks, out_rows):
    bsz, t, _ = k_arr.shape
    assert n_groups % n_sub == 0 and k_col0 % n_sub == 0 and v_col0 % n_sub == 0 and n_lat % tq == 0
    with_ctx = out_rows > n_lat
    kern = functools.partial(_flash_kernel, n_sub=n_sub, n_split=n_split, n_heads=n_heads, dk=dk, dv=dv, tq_lat=tq,
                             lat_chunks=lat_chunks,
                             ctx_rows=(n_lat, out_rows - n_lat) if with_ctx else None, ctx_chunks=ctx_chunks)
    return pl.pallas_call(
        kern,
        grid=(bsz, n_groups // n_sub, n_lat // tq + int(with_ctx)),
        in_specs=[
            pl.BlockSpec((None, t, n_sub * n_heads * dk), lambda b, g, i: (b, 0, g)),
            pl.BlockSpec((None, t, n_sub * dk), lambda b, g, i: (b, 0, k_col0 // n_sub + g)),
            pl.BlockSpec((None, t, n_sub * dv), lambda b, g, i: (b, 0, v_col0 // n_sub + g)),
        ],
        out_specs=pl.BlockSpec((None, out_rows, n_sub * n_heads * dv), lambda b, g, i: (b, 0, g)),
        out_shape=jax.ShapeDtypeStruct((bsz, out_rows, n_groups * n_heads * dv), BF16),
        compiler_params=_cparams(3, VMEM_LIMIT),
        name=name,
    )(q_arr, k_arr, v_arr)


def _na_kernel(q_ref, k_ref, v_ref, bias_ref, o_ref, *, n_lat, n_ctx, n_groups, n_ctx_steps):
    gi = pl.program_id(2) - n_ctx_steps
    rows = n_groups * NA_Q_ROWS
    tq = NA_Q_ROWS * GRID_W
    tk = NA_K_ROWS * GRID_W
    lo = _pair_masks((tq, LANES))

    def stacked_q(cols):
        qb = q_ref[:, cols]
        zero = jnp.zeros_like(qb)
        return jnp.concatenate([jnp.where(lo, qb, zero), jnp.where(lo, zero, qb)], axis=0)

    def with_ones(v):
        return jnp.concatenate([v, jnp.ones(v.shape, v.dtype)], axis=1)

    def finish(o1, cols):
        o = o1[:, :LANES] / o1[:, LANES:]
        o_ref[:, cols] = jnp.where(lo, o[:tq], o[tq:]).astype(o_ref.dtype)

    @pl.when(gi >= 0)
    def _():
        start = jnp.clip(gi * NA_Q_ROWS - B_MAX_KH // 2, 0, rows - NA_K_ROWS) * GRID_W
        start = pl.multiple_of(start, GRID_W)
        scores = []
        for s in range(NA_PAIRS_PER_STEP):
            cols = slice(s * LANES, (s + 1) * LANES)
            kk = jnp.concatenate([k_ref[pl.ds(start, tk), cols], k_ref[n_lat:n_lat + n_ctx, cols]], axis=0)
            scores.append(_dot_nt(stacked_q(cols), kk))
        for s in range(NA_PAIRS_PER_STEP):
            cols = slice(s * LANES, (s + 1) * LANES)
            vv = with_ones(jnp.concatenate([v_ref[pl.ds(start, tk), cols], v_ref[n_lat:n_lat + n_ctx, cols]], axis=0))
            bias = bias_ref[2 * s:2 * s + 2].reshape(2 * tq, tk)
            sc = jnp.concatenate([scores[s][:, :tk] + bias, scores[s][:, tk:]], axis=1)
            p = jnp.exp2(sc - jnp.max(sc, axis=-1, keepdims=True))
            finish(jnp.dot(p.astype(BF16), vv, preferred_element_type=F32), cols)

    @pl.when(gi < 0)
    def _():
        for s in range(NA_PAIRS_PER_STEP):
            cols = slice(s * LANES, (s + 1) * LANES)
            q = stacked_q(cols)
            s_c = _dot_nt(q, k_ref[n_lat:n_lat + n_ctx, cols])
            p_c = jnp.exp2(s_c - jnp.max(s_c, axis=-1, keepdims=True))
            vc = with_ones(v_ref[n_lat:n_lat + n_ctx, cols])
            finish(jnp.dot(p_c.astype(BF16), vc, preferred_element_type=F32), cols)


def _na_call(qkv, bias, n_lat, n_ctx, out_rows):
    bsz, t, w3 = qkv.shape
    d = w3 // 3
    w = NA_PAIRS_PER_STEP * LANES
    n_steps = d // w
    tq = NA_Q_ROWS * GRID_W
    tk = NA_K_ROWS * GRID_W
    n_groups = n_lat // tq
    n_ctx_steps = out_rows // tq - n_groups
    kern = functools.partial(_na_kernel, n_lat=n_lat, n_ctx=n_ctx, n_groups=n_groups, n_ctx_steps=n_ctx_steps)

    def row_tile(g):
        return jnp.where(g < n_ctx_steps, n_groups, g - n_ctx_steps)

    def bias_map(b, p, g):
        gi = g - n_ctx_steps
        case = jnp.where(gi <= 0, 0, jnp.where(gi == n_groups - 1, 2, 1))
        return (p, case, 0, 0)

    assert n_ctx == tq and n_ctx_steps in (0, 1)
    return pl.pallas_call(
        kern,
        grid=(bsz, n_steps, out_rows // tq),
        in_specs=[
            pl.BlockSpec((None, tq, w), lambda b, p, g: (b, row_tile(g), p)),
            pl.BlockSpec((None, t, w), lambda b, p, g: (b, 0, n_steps + p)),
            pl.BlockSpec((None, t, w), lambda b, p, g: (b, 0, 2 * n_steps + p)),
            pl.BlockSpec((2 * NA_PAIRS_PER_STEP, None, tq, tk), bias_map),
        ],
        out_specs=pl.BlockSpec((None, tq, w), lambda b, p, g: (b, row_tile(g), p)),
        out_shape=jax.ShapeDtypeStruct((bsz, out_rows, d), BF16),
        compiler_params=_cparams(3, 48 * 1024 * 1024),
        name="attn_na",
    )(qkv, qkv, qkv, bias)


def _post_kernel(o_ref, x_ref, mod_ref, wo_ref, wg_ref, wu_ref, wd_ref, lng_ref, lnb_ref, out_ref,
                 *, n_lat_tiles, ctx_row, alpha):
    nb, tm, d = x_ref.shape
    rows = _mod_rows(nb, n_lat_tiles, ctx_row)
    o = o_ref[...]
    y = jnp.dot(o.reshape(nb * tm, o.shape[-1]), wo_ref[...], preferred_element_type=F32)
    x1s, hs = [], []
    for s, row in enumerate(rows):
        g1 = _mod_vec(mod_ref, row, 2, d)
        sh2 = _mod_vec(mod_ref, row, 3, d)
        sc2 = _mod_vec(mod_ref, row, 4, d)
        x1 = _layer_norm(alpha * x_ref[s] + g1 * y[s * tm:(s + 1) * tm], lng_ref[0:1, :], lnb_ref[0:1, :])
        x1s.append(x1)
        hs.append((x1 * (1.0 + sc2) + sh2).astype(BF16))
    h = jnp.concatenate(hs, axis=0)
    gate = jnp.dot(h, wg_ref[...], preferred_element_type=F32)
    up = jnp.dot(h, wu_ref[...], preferred_element_type=F32)
    a = (gate / (1.0 + jnp.exp(-gate)) * up).astype(BF16)
    f = jnp.dot(a, wd_ref[...], preferred_element_type=F32)
    for s, row in enumerate(rows):
        g2 = _mod_vec(mod_ref, row, 5, d)
        out_ref[s] = _layer_norm(alpha * x1s[s] + g2 * f[s * tm:(s + 1) * tm], lng_ref[1:2, :], lnb_ref[1:2, :])


def _post_call(o, xa, mod, layer, wo_all, wo_idx, wg_all, wu_all, wd_all, lng, lnb, n_lat, out_rows, alpha):
    bsz, _, d = xa.shape
    tm = ROW_TILE
    d_ff = wg_all.shape[2]
    kern = functools.partial(_post_kernel, n_lat_tiles=n_lat // tm, ctx_row=bsz, alpha=alpha)
    return pl.pallas_call(
        kern,
        grid=(bsz // ROW_BATCHES, out_rows // tm),
        in_specs=[
            _row_block(ROW_BATCHES, tm, o.shape[-1]),
            _row_block(ROW_BATCHES, tm, d),
            pl.BlockSpec((None, MOD_ROWS, 6 * d), lambda b, i: (layer, 0, 0)),
            _resident((None,) + wo_all.shape[1:], lambda b, i: (wo_idx, 0, 0)),
            _resident((None, d, d_ff), lambda b, i: (layer, 0, 0)),
            _resident((None, d, d_ff), lambda b, i: (layer, 0, 0)),
            _resident((None, d_ff, d), lambda b, i: (layer, 0, 0)),
            pl.BlockSpec((None, 2, d), lambda b, i: (layer, 0, 0)),
            pl.BlockSpec((None, 2, d), lambda b, i: (layer, 0, 0)),
        ],
        out_specs=_row_block(ROW_BATCHES, tm, d),
        out_shape=jax.ShapeDtypeStruct((bsz, out_rows, d), F32),
        compiler_params=_cparams(2, VMEM_LIMIT),
        name="post_ffn",
    )(o, xa, mod, wo_all, wg_all, wu_all, wd_all, lng, lnb)


def _rope_tables(n_lat, n_ctx, n_freq):
    half = LANES // 2
    t = np.arange(n_lat)
    freqs = ROPE_BASE ** (-np.arange(0, 2 * n_freq, 2, dtype=np.float64) / (2 * n_freq))
    ang = np.concatenate([(t // GRID_W)[:, None] * freqs[None, :], (t % GRID_W)[:, None] * freqs[None, :]], axis=1)
    cos = np.zeros((n_lat + n_ctx, LANES))
    sin = np.zeros((n_lat + n_ctx, LANES))
    w = 2 * n_freq
    cos[:n_lat, :w] = np.cos(ang)
    cos[:n_lat, half:half + w] = np.cos(ang)
    sin[:n_lat, :w] = -np.sin(ang)
    sin[:n_lat, half:half + w] = np.sin(ang)
    cos[n_lat:, :w] = 1.0
    cos[n_lat:, half:half + w] = 1.0
    return jnp.asarray(cos, F32), jnp.asarray(sin, F32)


def _rope_perm(n_freq):
    idx = np.full((LANES,), -1, np.int64)
    f = n_freq
    idx[0:f] = np.arange(0, f)
    idx[f:2 * f] = np.arange(2 * f, 3 * f)
    idx[64:64 + f] = np.arange(f, 2 * f)
    idx[64 + f:64 + 2 * f] = np.arange(3 * f, 4 * f)
    return idx


def _take_cols(w, idx):
    wz = jnp.concatenate([w, jnp.zeros((w.shape[0], 1), w.dtype)], axis=1)
    return wz[:, np.where(idx < 0, w.shape[1], idx)]


def _na_row_cases(rows):
    kh = B_MAX_KH
    i = np.arange(NA_Q_ROWS)[:, None]
    j = np.arange(NA_K_ROWS)[None, :]
    cases = []
    for r0 in (0, NA_Q_ROWS, rows - NA_Q_ROWS):
        start = int(np.clip(r0 - kh // 2, 0, rows - NA_K_ROWS))
        rs = np.clip(r0 + i - kh // 2, 0, rows - kh)
        key_row = start + j
        cases.append(((key_row >= rs) & (key_row < rs + kh), key_row - (r0 + i) + (kh - 1)))
    return np.stack([c[0] for c in cases]), np.stack([c[1] for c in cases])


def _na_bias_kernel(r2_ref, o_ref, *, row_ok, dr_idx):
    n_e = r2_ref.shape[0]
    shape = (GRID_W, LANES)
    c = lax.broadcasted_iota(jnp.int32, shape, 0)
    lane = lax.broadcasted_iota(jnp.int32, shape, 1)
    kc = jnp.where(lane < GRID_W, lane, lane - GRID_W)
    cs = jnp.clip(c - B_KW // 2, 0, GRID_W - B_KW)
    col_ok = (kc >= cs) & (kc < cs + B_KW)
    left = lane < GRID_W
    neg = jnp.full(shape, NEG_BIG, F32)
    x = r2_ref[...]
    tiles = []
    for e in range(n_e):
        t = pltpu.roll(jnp.broadcast_to(x[e:e + 1, :], shape), LANES - (B_KW - 1), 1, stride=1, stride_axis=0)
        tiles.append(jnp.where(col_ok, t * LOG2E, neg))
    for case in range(row_ok.shape[0]):
        for i in range(NA_Q_ROWS):
            for jp in range(NA_K_ROWS // 2):
                ok_l, ok_r = bool(row_ok[case, i, 2 * jp]), bool(row_ok[case, i, 2 * jp + 1])
                e = int(np.clip(dr_idx[case, i, 2 * jp] + 1, 0, n_e - 1))
                if ok_l and ok_r:
                    tile = tiles[e]
                elif ok_l:
                    tile = jnp.where(left, tiles[e], neg)
                elif ok_r:
                    tile = jnp.where(left, neg, tiles[e])
                else:
                    tile = neg
                o_ref[case, i * GRID_W:(i + 1) * GRID_W, jp * LANES:(jp + 1) * LANES] = tile


def _na_bias_tables(rpb, rows):
    n_heads, n_dr, n_dc = rpb.shape
    n_e = n_dr + 1
    r2 = jnp.zeros((n_heads, n_e, LANES), F32)
    r2 = r2.at[:, 1:, :n_dc].set(rpb).at[:, :n_dr, GRID_W:GRID_W + n_dc].set(rpb)
    row_ok, dr_idx = _na_row_cases(rows)
    kern = functools.partial(_na_bias_kernel, row_ok=row_ok, dr_idx=dr_idx)
    tq, tk = NA_Q_ROWS * GRID_W, NA_K_ROWS * GRID_W
    return pl.pallas_call(
        kern,
        grid=(n_heads,),
        in_specs=[pl.BlockSpec((None, n_e, LANES), lambda h: (h, 0, 0))],
        out_specs=pl.BlockSpec((None, 3, tq, tk), lambda h: (h, 0, 0, 0)),
        out_shape=jax.ShapeDtypeStruct((n_heads, 3, tq, tk), F32),
        compiler_params=_cparams(1, 32 * 1024 * 1024),
        name="na_bias",
    )(r2)


def kernel(x, c, ctx, c_ctx, w_ada, b_ada, ln_g, ln_b, w_ffn_gate, w_ffn_up, w_ffn_down, a_w_qkv, a_q_gain, a_k_gain, a_w_o, b_w_qkv, b_rpb, b_w_o, c_w_dqkv, c_q_a_gain, c_kv_a_gain, c_w_uq, c_w_ukv, c_w_o):
    bsz, n_lat, d = x.shape
    n_ctx = ctx.shape[1]
    t = n_lat + n_ctx
    depth = w_ada.shape[0]
    rows = n_lat // GRID_W
    assert n_lat % ROW_TILE == 0 and n_ctx == ROW_TILE and bsz < MOD_ROWS and bsz % ROW_BATCHES == 0
    assert n_lat % (NA_Q_ROWS * GRID_W) == 0 and rows >= NA_K_ROWS
    assert a_w_qkv.shape[2] == (A_HEADS + 2 * A_KV_HEADS) * A_HEAD_DIM and b_rpb.shape[1] == B_HEADS
    assert c_w_dqkv.shape[2] == C_Q_RANK + C_KV_RANK + C_ROPE and d == B_HEADS * B_HEAD_DIM
    alpha = (2.0 * depth) ** 0.25

    xa = jnp.concatenate([x, ctx], axis=1)
    cond = jnp.zeros((MOD_ROWS, d), F32).at[:bsz].set(c).at[bsz].set(c_ctx)
    mod = _ada_call(cond, w_ada, b_ada)

    cos_a, sin_a = _rope_tables(n_lat, n_ctx, A_HEAD_DIM // 4)
    cos_c, sin_c = _rope_tables(n_lat, n_ctx, C_ROPE // 4)
    perm_a = _rope_perm(A_HEAD_DIM // 4)
    perm_c = _rope_perm(C_ROPE // 4)

    wg_all, wu_all, wd_all = w_ffn_gate.astype(BF16), w_ffn_up.astype(BF16), w_ffn_down.astype(BF16)
    wo_a, wo_b, wo_c = a_w_o.astype(BF16), b_w_o.astype(BF16), c_w_o.astype(BF16)

    ctx_chunks = ((n_lat, n_ctx),)
    lat_chunks = tuple((s, 1024) for s in range(0, n_lat, 1024)) + ctx_chunks

    for i in range(depth):
        last = i == depth - 1
        out_rows = n_lat if last else t
        kind, j = i % 3, i // 3
        if kind == 0:
            hd = A_HEAD_DIM
            nqk = A_HEADS + A_KV_HEADS
            col_idx = np.concatenate([h * hd + perm_a for h in range(nqk)]
                                     + [np.arange(nqk * hd, (nqk + A_KV_HEADS) * hd)])
            w = a_w_qkv[j][:, col_idx].astype(BF16)
            qkv = _qkv_a_call(xa, mod, i, w, a_q_gain[j][perm_a][None, :], a_k_gain[j][perm_a][None, :],
                              cos_a, sin_a, n_lat)
            o = _flash_call(qkv, qkv, qkv, name="attn_gqa", n_groups=A_KV_HEADS, n_sub=1, n_split=2,
                            n_heads=A_HEADS // A_KV_HEADS, dk=hd, dv=hd, tq=256, k_col0=A_HEADS,
                            v_col0=A_HEADS + A_KV_HEADS, n_lat=n_lat, lat_chunks=lat_chunks,
                            ctx_chunks=ctx_chunks, out_rows=out_rows)
            wo_all = wo_a
        elif kind == 1:
            qkv = _qkv_b_call(xa, mod, i, b_w_qkv[j].astype(BF16), n_lat)
            o = _na_call(qkv, _na_bias_tables(b_rpb[j], rows), n_lat, n_ctx, out_rows)
            wo_all = wo_b
        else:
            r1 = C_Q_RANK + C_KV_RANK
            wd_idx = np.concatenate([np.arange(r1), np.where(perm_c < 0, -1, r1 + perm_c)])
            wdn = _take_cols(c_w_dqkv[j], wd_idx).astype(BF16)
            hq = C_NOPE + C_ROPE
            uq_idx = np.concatenate([np.concatenate([h * hq + np.arange(C_NOPE),
                                                     np.where(perm_c < 0, -1, h * hq + C_NOPE + perm_c)])
                                     for h in range(C_HEADS)])
            wuq = _take_cols(c_w_uq[j], uq_idx).astype(BF16)
            qc, kc, vc = _proj_c_call(xa, mod, i, wdn, c_q_a_gain[j][None, :], c_kv_a_gain[j][None, :], wuq,
                                      c_w_ukv[j].astype(BF16), cos_c, sin_c, n_lat)
            o = _flash_call(qc, kc, vc, name="attn_mla", n_groups=C_HEADS, n_sub=1, n_split=2, n_heads=1, dk=2 * LANES,
                            dv=C_V, tq=1024, k_col0=0, v_col0=0, n_lat=n_lat, lat_chunks=lat_chunks,
                            ctx_chunks=ctx_chunks, out_rows=out_rows)
            wo_all = wo_c
        xa = _post_call(o, xa, mod, i, wo_all, j, wg_all, wu_all, wd_all, ln_g, ln_b, n_lat, out_rows, alpha)
    return xa
```

```python
import functools
import math

import numpy as np
import jax
import jax.numpy as jnp
from jax import lax
from jax.experimental import pallas as pl
from jax.experimental.pallas import tpu as pltpu

GRID_W = 64
A_HEADS, A_KV_HEADS, A_HEAD_DIM = 8, 2, 128
B_HEADS, B_HEAD_DIM, B_MAX_KH, B_KW = 16, 64, 8, 16
C_HEADS, C_NOPE, C_ROPE, C_V, C_Q_RANK, C_KV_RANK = 8, 128, 64, 128, 384, 256
ROPE_BASE = 10000.0
RMS_EPS = 1e-6
LN_EPS = 1e-5
LOG2E = math.log2(math.e)
NEG_BIG = -1e30

LANES = 128
V7X_VMEM_BYTES = 64 * 1024 * 1024
VMEM_LIMIT = 56 * 1024 * 1024

ROW_TILE = 256
ROW_BATCHES = 2
MOD_ROWS = 8
NA_Q_ROWS = 4
NA_K_ROWS = 12
NA_PAIRS_PER_STEP = 4

BF16 = jnp.bfloat16
F32 = jnp.float32


def _cparams(n_axes, vmem=None):
    return pltpu.CompilerParams(dimension_semantics=("arbitrary",) * n_axes, vmem_limit_bytes=vmem)


def _resident(shape, index_map):
    return pl.BlockSpec(shape, index_map, pipeline_mode=pl.Buffered(1))


def _ada_kernel(c_ref, w_ref, b_ref, o_ref):
    c = c_ref[...]
    cs = c / (1.0 + jnp.exp(-c))
    o_ref[...] = jnp.dot(cs, w_ref[...], preferred_element_type=F32) + b_ref[...]


def _ada_call(cond, w_ada, b_ada):
    depth, d, n6 = w_ada.shape
    tn = n6 // 4
    return pl.pallas_call(
        _ada_kernel,
        grid=(depth, n6 // tn),
        in_specs=[
            pl.BlockSpec((MOD_ROWS, d), lambda l, j: (0, 0)),
            pl.BlockSpec((None, d, tn), lambda l, j: (l, 0, j)),
            pl.BlockSpec((None, 1, tn), lambda l, j: (l, 0, j)),
        ],
        out_specs=pl.BlockSpec((None, MOD_ROWS, tn), lambda l, j: (l, 0, j)),
        out_shape=jax.ShapeDtypeStruct((depth, MOD_ROWS, n6), F32),
        compiler_params=_cparams(2, 40 * 1024 * 1024),
        name="ada_mod",
    )(cond, w_ada, b_ada.reshape(depth, 1, n6))


def _mod_rows(nb, n_lat_tiles, ctx_row):
    b, i = pl.program_id(0), pl.program_id(1)
    return [jnp.where(i < n_lat_tiles, b * nb + s, ctx_row) for s in range(nb)]


def _mod_vec(mod_ref, row, k, d):
    return mod_ref[pl.ds(row, 1), k * d:(k + 1) * d]


def _modulated(x_ref, mod_ref, rows, k_shift, d):
    parts = []
    for s, row in enumerate(rows):
        sh = _mod_vec(mod_ref, row, k_shift, d)
        sc = _mod_vec(mod_ref, row, k_shift + 1, d)
        parts.append((x_ref[s] * (1.0 + sc) + sh).astype(BF16))
    return jnp.concatenate(parts, axis=0)


def _store_rows(o_ref, cols, val):
    nb, tm, _ = o_ref.shape
    o_ref[:, :, cols] = val.astype(o_ref.dtype).reshape(nb, tm, val.shape[-1])


def _row_block(nb, tm, w):
    return pl.BlockSpec((nb, tm, w), lambda b, i: (b, i, 0))


def _table_rows(ref, nb):
    t = ref[...]
    return t if nb == 1 else jnp.concatenate([t] * nb, axis=0)


def _rms(t, g):
    return t * lax.rsqrt(jnp.mean(t * t, axis=-1, keepdims=True) + RMS_EPS) * g


def _rope(t, cos, sin):
    return t * cos + pltpu.roll(t, LANES // 2, 1) * sin


def _layer_norm(t, g, b):
    mu = jnp.mean(t, axis=-1, keepdims=True)
    c = t - mu
    var = jnp.mean(c * c, axis=-1, keepdims=True)
    return c * lax.rsqrt(var + LN_EPS) * g + b


def _qkv_a_kernel(x_ref, mod_ref, w_ref, gq_ref, gk_ref, cos_ref, sin_ref, o_ref, *, n_lat_tiles, ctx_row, q_scale):
    nb, _, d = x_ref.shape
    h = _modulated(x_ref, mod_ref, _mod_rows(nb, n_lat_tiles, ctx_row), 0, d)
    acc = jnp.dot(h, w_ref[...], preferred_element_type=F32)
    cos, sin = _table_rows(cos_ref, nb), _table_rows(sin_ref, nb)
    gq = gq_ref[...] * q_scale
    gk = gk_ref[...]
    hd = A_HEAD_DIM
    for j in range(A_HEADS + A_KV_HEADS):
        t = acc[:, j * hd:(j + 1) * hd]
        y = _rope(_rms(t, gq if j < A_HEADS else gk), cos, sin)
        _store_rows(o_ref, slice(j * hd, (j + 1) * hd), y)
    v0 = (A_HEADS + A_KV_HEADS) * hd
    _store_rows(o_ref, slice(v0, acc.shape[1]), acc[:, v0:])


def _qkv_a_call(xa, mod, layer, w, gq, gk, cos, sin, n_lat):
    bsz, t, d = xa.shape
    n_out = w.shape[1]
    tm = ROW_TILE
    kern = functools.partial(_qkv_a_kernel, n_lat_tiles=n_lat // tm, ctx_row=bsz,
                             q_scale=A_HEAD_DIM ** -0.5 * LOG2E)
    nb = 1
    return pl.pallas_call(
        kern,
        grid=(bsz // nb, t // tm),
        in_specs=[
            _row_block(nb, tm, d),
            pl.BlockSpec((None, MOD_ROWS, 6 * d), lambda b, i: (layer, 0, 0)),
            _resident((d, n_out), lambda b, i: (0, 0)),
            pl.BlockSpec((1, A_HEAD_DIM), lambda b, i: (0, 0)),
            pl.BlockSpec((1, A_HEAD_DIM), lambda b, i: (0, 0)),
            pl.BlockSpec((tm, LANES), lambda b, i: (i, 0)),
            pl.BlockSpec((tm, LANES), lambda b, i: (i, 0)),
        ],
        out_specs=_row_block(nb, tm, n_out),
        out_shape=jax.ShapeDtypeStruct((bsz, t, n_out), BF16),
        compiler_params=_cparams(2, 32 * 1024 * 1024),
        name="qkv_gqa",
    )(xa, mod, w, gq, gk, cos, sin)


def _qkv_b_kernel(x_ref, mod_ref, w_ref, o_ref, *, n_lat_tiles, ctx_row, q_scale):
    nb, _, d = x_ref.shape
    h = _modulated(x_ref, mod_ref, _mod_rows(nb, n_lat_tiles, ctx_row), 0, d)
    acc = jnp.dot(h, w_ref[...], preferred_element_type=F32)
    _store_rows(o_ref, slice(0, d), acc[:, :d] * q_scale)
    _store_rows(o_ref, slice(d, acc.shape[1]), acc[:, d:])


def _qkv_b_call(xa, mod, layer, w, n_lat):
    bsz, t, d = xa.shape
    n_out = w.shape[1]
    tm = ROW_TILE
    kern = functools.partial(_qkv_b_kernel, n_lat_tiles=n_lat // tm, ctx_row=bsz,
                             q_scale=B_HEAD_DIM ** -0.5 * LOG2E)
    return pl.pallas_call(
        kern,
        grid=(bsz // ROW_BATCHES, t // tm),
        in_specs=[
            _row_block(ROW_BATCHES, tm, d),
            pl.BlockSpec((None, MOD_ROWS, 6 * d), lambda b, i: (layer, 0, 0)),
            _resident((d, n_out), lambda b, i: (0, 0)),
        ],
        out_specs=_row_block(ROW_BATCHES, tm, n_out),
        out_shape=jax.ShapeDtypeStruct((bsz, t, n_out), BF16),
        compiler_params=_cparams(2, 40 * 1024 * 1024),
        name="qkv_na",
    )(xa, mod, w)


def _proj_c_kernel(x_ref, mod_ref, wd_ref, gq_ref, gkv_ref, wuq_ref, wukv_ref, cos_ref, sin_ref,
                   q_ref, k_ref, v_ref, *, n_lat_tiles, ctx_row, q_scale):
    nb, _, d = x_ref.shape
    h = _modulated(x_ref, mod_ref, _mod_rows(nb, n_lat_tiles, ctx_row), 0, d)
    cos, sin = _table_rows(cos_ref, nb), _table_rows(sin_ref, nb)
    dn = jnp.dot(h, wd_ref[...], preferred_element_type=F32)
    r0, r1 = C_Q_RANK, C_Q_RANK + C_KV_RANK
    ql = _rms(dn[:, :r0], gq_ref[...]).astype(BF16)
    kvl = _rms(dn[:, r0:r1], gkv_ref[...]).astype(BF16)
    kr = _rope(dn[:, r1:], cos, sin)
    q = jnp.dot(ql, wuq_ref[...], preferred_element_type=F32)
    kv = jnp.dot(kvl, wukv_ref[...], preferred_element_type=F32)
    w2 = 2 * LANES
    for hh in range(C_HEADS):
        nope, rot = slice(hh * w2, hh * w2 + LANES), slice(hh * w2 + LANES, (hh + 1) * w2)
        _store_rows(q_ref, nope, q[:, nope] * q_scale)
        _store_rows(q_ref, rot, _rope(q[:, rot], cos, sin) * q_scale)
        _store_rows(k_ref, nope, kv[:, nope])
        _store_rows(k_ref, rot, kr)
        _store_rows(v_ref, slice(hh * LANES, (hh + 1) * LANES), kv[:, rot])


def _proj_c_call(xa, mod, layer, wd, gq, gkv, wuq, wukv, cos, sin, n_lat):
    bsz, t, d = xa.shape
    tm = ROW_TILE
    kern = functools.partial(_proj_c_kernel, n_lat_tiles=n_lat // tm, ctx_row=bsz,
                             q_scale=(C_NOPE + C_ROPE) ** -0.5 * LOG2E)
    const = lambda b, i: (0, 0)
    qk_w = C_HEADS * 2 * LANES
    return pl.pallas_call(
        kern,
        grid=(bsz // ROW_BATCHES, t // tm),
        in_specs=[
            _row_block(ROW_BATCHES, tm, d),
            pl.BlockSpec((None, MOD_ROWS, 6 * d), lambda b, i: (layer, 0, 0)),
            _resident(wd.shape, const),
            pl.BlockSpec(gq.shape, const),
            pl.BlockSpec(gkv.shape, const),
            _resident(wuq.shape, const),
            _resident(wukv.shape, const),
            pl.BlockSpec((tm, LANES), lambda b, i: (i, 0)),
            pl.BlockSpec((tm, LANES), lambda b, i: (i, 0)),
        ],
        out_specs=[_row_block(ROW_BATCHES, tm, qk_w), _row_block(ROW_BATCHES, tm, qk_w),
                   _row_block(ROW_BATCHES, tm, C_HEADS * C_V)],
        out_shape=[
            jax.ShapeDtypeStruct((bsz, t, qk_w), BF16),
            jax.ShapeDtypeStruct((bsz, t, qk_w), BF16),
            jax.ShapeDtypeStruct((bsz, t, C_HEADS * C_V), BF16),
        ],
        compiler_params=_cparams(2, 40 * 1024 * 1024),
        name="proj_mla",
    )(xa, mod, wd, gq, gkv, wuq, wukv, cos, sin)


def _pair_masks(shape):
    lane = lax.broadcasted_iota(jnp.int32, shape, 1)
    return lane < (LANES // 2)


def _dot_nt(a, b):
    return lax.dot_general(a, b, (((1,), (1,)), ((), ())), preferred_element_type=F32)


def _flash_kernel(q_ref, k_ref, v_ref, o_ref, *, n_sub, n_split, n_heads, dk, dv, tq_lat, lat_chunks,
                  ctx_rows, ctx_chunks):
    def run(row0, tq, chunks):
        m_rows = n_heads * tq // n_split
        rows = pl.ds(row0, tq)
        chains, qs, state = [], [], []
        for u in range(n_sub):
            q0 = u * n_heads * dk
            q = jnp.concatenate([q_ref[rows, q0 + r * dk:q0 + (r + 1) * dk] for r in range(n_heads)], axis=0)
            for h in range(n_split):
                chains.append(u)
                qs.append(q[h * m_rows:(h + 1) * m_rows])
                state.append((jnp.full((m_rows, 1), NEG_BIG, F32), jnp.zeros((m_rows, dv + LANES), F32)))
        for start, size in chunks:
            ones = jnp.ones((size, LANES), BF16)
            scores = [_dot_nt(qs[c], k_ref[start:start + size, u * dk:(u + 1) * dk])
                      for c, u in enumerate(chains)]
            for c, u in enumerate(chains):
                m, acc = state[c]
                v1 = jnp.concatenate([v_ref[start:start + size, u * dv:(u + 1) * dv], ones], axis=1)
                m_new = jnp.maximum(m, jnp.max(scores[c], axis=-1, keepdims=True))
                alpha = jnp.exp2(m - m_new)
                p = jnp.exp2(scores[c] - m_new)
                acc = alpha * acc + jnp.dot(p.astype(BF16), v1, preferred_element_type=F32)
                state[c] = (m_new, acc)
        for u in range(n_sub):
            accs = [state[u * n_split + h][1] for h in range(n_split)]
            acc = accs[0] if n_split == 1 else jnp.concatenate(accs, axis=0)
            o = acc[:, :dv] / acc[:, dv:]
            o0 = u * n_heads * dv
            for r in range(n_heads):
                o_ref[rows, o0 + r * dv:o0 + (r + 1) * dv] = o[r * tq:(r + 1) * tq].astype(o_ref.dtype)

    n_ctx_steps = 0 if ctx_rows is None else 1
    i = pl.program_id(2) - n_ctx_steps

    @pl.when(i >= 0)
    def _():
        run(pl.multiple_of(i * tq_lat, tq_lat), tq_lat, lat_chunks)

    if ctx_rows is not None:
        @pl.when(i < 0)
        def _():
            run(ctx_rows[0], ctx_rows[1], ctx_chunks)


def _flash_call(q_arr, k_arr, v_arr, *, name, n_groups, n_sub, n_split, n_heads, dk, dv, tq, k_col0, v_col0, n_lat,
                lat_chunks, ctx_chunks, out_rows):
    bsz, t, _ = k_arr.shape
    assert n_groups % n_sub == 0 and k_col0 % n_sub == 0 and v_col0 % n_sub == 0 and n_lat % tq == 0
    with_ctx = out_rows > n_lat
    kern = functools.partial(_flash_kernel, n_sub=n_sub, n_split=n_split, n_heads=n_heads, dk=dk, dv=dv, tq_lat=tq,
                             lat_chunks=lat_chunks,
                             ctx_rows=(n_lat, out_rows - n_lat) if with_ctx else None, ctx_chunks=ctx_chunks)
    return pl.pallas_call(
        kern,
        grid=(bsz, n_groups // n_sub, n_lat // tq + int(with_ctx)),
        in_specs=[
            pl.BlockSpec((None, t, n_sub * n_heads * dk), lambda b, g, i: (b, 0, g)),
            pl.BlockSpec((None, t, n_sub * dk), lambda b, g, i: (b, 0, k_col0 // n_sub + g)),
            pl.BlockSpec((None, t, n_sub * dv), lambda b, g, i: (b, 0, v_col0 // n_sub + g)),
        ],
        out_specs=pl.BlockSpec((None, out_rows, n_sub * n_heads * dv), lambda b, g, i: (b, 0, g)),
        out_shape=jax.ShapeDtypeStruct((bsz, out_rows, n_groups * n_heads * dv), BF16),
        compiler_params=_cparams(3, VMEM_LIMIT),
        name=name,
    )(q_arr, k_arr, v_arr)


def _na_kernel(q_ref, k_ref, v_ref, bias_ref, o_ref, *, n_lat, n_ctx, n_groups, n_ctx_steps):
    gi = pl.program_id(2) - n_ctx_steps
    rows = n_groups * NA_Q_ROWS
    tq = NA_Q_ROWS * GRID_W
    tk = NA_K_ROWS * GRID_W
    lo = _pair_masks((tq, LANES))

    def stacked_q(cols):
        qb = q_ref[:, cols]
        zero = jnp.zeros_like(qb)
        return jnp.concatenate([jnp.where(lo, qb, zero), jnp.where(lo, zero, qb)], axis=0)

    def with_ones(v):
        return jnp.concatenate([v, jnp.ones(v.shape, v.dtype)], axis=1)

    def finish(o1, cols):
        o = o1[:, :LANES] / o1[:, LANES:]
        o_ref[:, cols] = jnp.where(lo, o[:tq], o[tq:]).astype(o_ref.dtype)

    @pl.when(gi >= 0)
    def _():
        start = jnp.clip(gi * NA_Q_ROWS - B_MAX_KH // 2, 0, rows - NA_K_ROWS) * GRID_W
        start = pl.multiple_of(start, GRID_W)
        scores = []
        for s in range(NA_PAIRS_PER_STEP):
            cols = slice(s * LANES, (s + 1) * LANES)
            kk = jnp.concatenate([k_ref[pl.ds(start, tk), cols], k_ref[n_lat:n_lat + n_ctx, cols]], axis=0)
            scores.append(_dot_nt(stacked_q(cols), kk))
        for s in range(NA_PAIRS_PER_STEP):
            cols = slice(s * LANES, (s + 1) * LANES)
            vv = with_ones(jnp.concatenate([v_ref[pl.ds(start, tk), cols], v_ref[n_lat:n_lat + n_ctx, cols]], axis=0))
            bias = bias_ref[2 * s:2 * s + 2].reshape(2 * tq, tk)
            sc = jnp.concatenate([scores[s][:, :tk] + bias, scores[s][:, tk:]], axis=1)
            p = jnp.exp2(sc - jnp.max(sc, axis=-1, keepdims=True))
            finish(jnp.dot(p.astype(BF16), vv, preferred_element_type=F32), cols)

    @pl.when(gi < 0)
    def _():
        for s in range(NA_PAIRS_PER_STEP):
            cols = slice(s * LANES, (s + 1) * LANES)
            q = stacked_q(cols)
            s_c = _dot_nt(q, k_ref[n_lat:n_lat + n_ctx, cols])
            p_c = jnp.exp2(s_c - jnp.max(s_c, axis=-1, keepdims=True))
            vc = with_ones(v_ref[n_lat:n_lat + n_ctx, cols])
            finish(jnp.dot(p_c.astype(BF16), vc, preferred_element_type=F32), cols)


def _na_call(qkv, bias, n_lat, n_ctx, out_rows):
    bsz, t, w3 = qkv.shape
    d = w3 // 3
    w = NA_PAIRS_PER_STEP * LANES
    n_steps = d // w
    tq = NA_Q_ROWS * GRID_W
    tk = NA_K_ROWS * GRID_W
    n_groups = n_lat // tq
    n_ctx_steps = out_rows // tq - n_groups
    kern = functools.partial(_na_kernel, n_lat=n_lat, n_ctx=n_ctx, n_groups=n_groups, n_ctx_steps=n_ctx_steps)

    def row_tile(g):
        return jnp.where(g < n_ctx_steps, n_groups, g - n_ctx_steps)

    def bias_map(b, p, g):
        gi = g - n_ctx_steps
        case = jnp.where(gi <= 0, 0, jnp.where(gi == n_groups - 1, 2, 1))
        return (p, case, 0, 0)

    assert n_ctx == tq and n_ctx_steps in (0, 1)
    return pl.pallas_call(
        kern,
        grid=(bsz, n_steps, out_rows // tq),
        in_specs=[
            pl.BlockSpec((None, tq, w), lambda b, p, g: (b, row_tile(g), p)),
            pl.BlockSpec((None, t, w), lambda b, p, g: (b, 0, n_steps + p)),
            pl.BlockSpec((None, t, w), lambda b, p, g: (b, 0, 2 * n_steps + p)),
            pl.BlockSpec((2 * NA_PAIRS_PER_STEP, None, tq, tk), bias_map),
        ],
        out_specs=pl.BlockSpec((None, tq, w), lambda b, p, g: (b, row_tile(g), p)),
        out_shape=jax.ShapeDtypeStruct((bsz, out_rows, d), BF16),
        compiler_params=_cparams(3, 48 * 1024 * 1024),
        name="attn_na",
    )(qkv, qkv, qkv, bias)


def _post_kernel(o_ref, x_ref, mod_ref, wo_ref, wg_ref, wu_ref, wd_ref, lng_ref, lnb_ref, out_ref,
                 *, n_lat_tiles, ctx_row, alpha):
    nb, tm, d = x_ref.shape
    rows = _mod_rows(nb, n_lat_tiles, ctx_row)
    o = o_ref[...]
    y = jnp.dot(o.reshape(nb * tm, o.shape[-1]), wo_ref[...], preferred_element_type=F32)
    x1s, hs = [], []
    for s, row in enumerate(rows):
        g1 = _mod_vec(mod_ref, row, 2, d)
        sh2 = _mod_vec(mod_ref, row, 3, d)
        sc2 = _mod_vec(mod_ref, row, 4, d)
        x1 = _layer_norm(alpha * x_ref[s] + g1 * y[s * tm:(s + 1) * tm], lng_ref[0:1, :], lnb_ref[0:1, :])
        x1s.append(x1)
        hs.append((x1 * (1.0 + sc2) + sh2).astype(BF16))
    h = jnp.concatenate(hs, axis=0)
    gate = jnp.dot(h, wg_ref[...], preferred_element_type=F32)
    up = jnp.dot(h, wu_ref[...], preferred_element_type=F32)
    a = (gate / (1.0 + jnp.exp(-gate)) * up).astype(BF16)
    f = jnp.dot(a, wd_ref[...], preferred_element_type=F32)
    for s, row in enumerate(rows):
        g2 = _mod_vec(mod_ref, row, 5, d)
        out_ref[s] = _layer_norm(alpha * x1s[s] + g2 * f[s * tm:(s + 1) * tm], lng_ref[1:2, :], lnb_ref[1:2, :])


def _post_call(o, xa, mod, layer, wo_all, wo_idx, wg_all, wu_all, wd_all, lng, lnb, n_lat, out_rows, alpha):
    bsz, _, d = xa.shape
    tm = ROW_TILE
    d_ff = wg_all.shape[2]
    kern = functools.partial(_post_kernel, n_lat_tiles=n_lat // tm, ctx_row=bsz, alpha=alpha)
    return pl.pallas_call(
        kern,
        grid=(bsz // ROW_BATCHES, out_rows // tm),
        in_specs=[
            _row_block(ROW_BATCHES, tm, o.shape[-1]),
            _row_block(ROW_BATCHES, tm, d),
            pl.BlockSpec((None, MOD_ROWS, 6 * d), lambda b, i: (layer, 0, 0)),
            _resident((None,) + wo_all.shape[1:], lambda b, i: (wo_idx, 0, 0)),
            _resident((None, d, d_ff), lambda b, i: (layer, 0, 0)),
            _resident((None, d, d_ff), lambda b, i: (layer, 0, 0)),
            _resident((None, d_ff, d), lambda b, i: (layer, 0, 0)),
            pl.BlockSpec((None, 2, d), lambda b, i: (layer, 0, 0)),
            pl.BlockSpec((None, 2, d), lambda b, i: (layer, 0, 0)),
        ],
        out_specs=_row_block(ROW_BATCHES, tm, d),
        out_shape=jax.ShapeDtypeStruct((bsz, out_rows, d), F32),
        compiler_params=_cparams(2, VMEM_LIMIT),
        name="post_ffn",
    )(o, xa, mod, wo_all, wg_all, wu_all, wd_all, lng, lnb)


def _rope_tables(n_lat, n_ctx, n_freq):
    half = LANES // 2
    t = np.arange(n_lat)
    freqs = ROPE_BASE ** (-np.arange(0, 2 * n_freq, 2, dtype=np.float64) / (2 * n_freq))
    ang = np.concatenate([(t // GRID_W)[:, None] * freqs[None, :], (t % GRID_W)[:, None] * freqs[None, :]], axis=1)
    cos = np.zeros((n_lat + n_ctx, LANES))
    sin = np.zeros((n_lat + n_ctx, LANES))
    w = 2 * n_freq
    cos[:n_lat, :w] = np.cos(ang)
    cos[:n_lat, half:half + w] = np.cos(ang)
    sin[:n_lat, :w] = -np.sin(ang)
    sin[:n_lat, half:half + w] = np.sin(ang)
    cos[n_lat:, :w] = 1.0
    cos[n_lat:, half:half + w] = 1.0
    return jnp.asarray(cos, F32), jnp.asarray(sin, F32)


def _rope_perm(n_freq):
    idx = np.full((LANES,), -1, np.int64)
    f = n_freq
    idx[0:f] = np.arange(0, f)
    idx[f:2 * f] = np.arange(2 * f, 3 * f)
    idx[64:64 + f] = np.arange(f, 2 * f)
    idx[64 + f:64 + 2 * f] = np.arange(3 * f, 4 * f)
    return idx


def _take_cols(w, idx):
    wz = jnp.concatenate([w, jnp.zeros((w.shape[0], 1), w.dtype)], axis=1)
    return wz[:, np.where(idx < 0, w.shape[1], idx)]


def _na_row_cases(rows):
    kh = B_MAX_KH
    i = np.arange(NA_Q_ROWS)[:, None]
    j = np.arange(NA_K_ROWS)[None, :]
    cases = []
    for r0 in (0, NA_Q_ROWS, rows - NA_Q_ROWS):
        start = int(np.clip(r0 - kh // 2, 0, rows - NA_K_ROWS))
        rs = np.clip(r0 + i - kh // 2, 0, rows - kh)
        key_row = start + j
        cases.append(((key_row >= rs) & (key_row < rs + kh), key_row - (r0 + i) + (kh - 1)))
    return np.stack([c[0] for c in cases]), np.stack([c[1] for c in cases])


def _na_bias_kernel(r2_ref, o_ref, *, row_ok, dr_idx):
    n_e = r2_ref.shape[0]
    shape = (GRID_W, LANES)
    c = lax.broadcasted_iota(jnp.int32, shape, 0)
    lane = lax.broadcasted_iota(jnp.int32, shape, 1)
    kc = jnp.where(lane < GRID_W, lane, lane - GRID_W)
    cs = jnp.clip(c - B_KW // 2, 0, GRID_W - B_KW)
    col_ok = (kc >= cs) & (kc < cs + B_KW)
    left = lane < GRID_W
    neg = jnp.full(shape, NEG_BIG, F32)
    x = r2_ref[...]
    tiles = []
    for e in range(n_e):
        t = pltpu.roll(jnp.broadcast_to(x[e:e + 1, :], shape), LANES - (B_KW - 1), 1, stride=1, stride_axis=0)
        tiles.append(jnp.where(col_ok, t * LOG2E, neg))
    for case in range(row_ok.shape[0]):
        for i in range(NA_Q_ROWS):
            for jp in range(NA_K_ROWS // 2):
                ok_l, ok_r = bool(row_ok[case, i, 2 * jp]), bool(row_ok[case, i, 2 * jp + 1])
                e = int(np.clip(dr_idx[case, i, 2 * jp] + 1, 0, n_e - 1))
                if ok_l and ok_r:
                    tile = tiles[e]
                elif ok_l:
                    tile = jnp.where(left, tiles[e], neg)
                elif ok_r:
                    tile = jnp.where(left, neg, tiles[e])
                else:
                    tile = neg
                o_ref[case, i * GRID_W:(i + 1) * GRID_W, jp * LANES:(jp + 1) * LANES] = tile


def _na_bias_tables(rpb, rows):
    n_heads, n_dr, n_dc = rpb.shape
    n_e = n_dr + 1
    r2 = jnp.zeros((n_heads, n_e, LANES), F32)
    r2 = r2.at[:, 1:, :n_dc].set(rpb).at[:, :n_dr, GRID_W:GRID_W + n_dc].set(rpb)
    row_ok, dr_idx = _na_row_cases(rows)
    kern = functools.partial(_na_bias_kernel, row_ok=row_ok, dr_idx=dr_idx)
    tq, tk = NA_Q_ROWS * GRID_W, NA_K_ROWS * GRID_W
    return pl.pallas_call(
        kern,
        grid=(n_heads,),
        in_specs=[pl.BlockSpec((None, n_e, LANES), lambda h: (h, 0, 0))],
        out_specs=pl.BlockSpec((None, 3, tq, tk), lambda h: (h, 0, 0, 0)),
        out_shape=jax.ShapeDtypeStruct((n_heads, 3, tq, tk), F32),
        compiler_params=_cparams(1, 32 * 1024 * 1024),
        name="na_bias",
    )(r2)


def kernel(x, c, ctx, c_ctx, w_ada, b_ada, ln_g, ln_b, w_ffn_gate, w_ffn_up, w_ffn_down, a_w_qkv, a_q_gain, a_k_gain, a_w_o, b_w_qkv, b_rpb, b_w_o, c_w_dqkv, c_q_a_gain, c_kv_a_gain, c_w_uq, c_w_ukv, c_w_o):
    bsz, n_lat, d = x.shape
    n_ctx = ctx.shape[1]
    t = n_lat + n_ctx
    depth = w_ada.shape[0]
    rows = n_lat // GRID_W
    assert n_lat % ROW_TILE == 0 and n_ctx == ROW_TILE and bsz < MOD_ROWS and bsz % ROW_BATCHES == 0
    assert n_lat % (NA_Q_ROWS * GRID_W) == 0 and rows >= NA_K_ROWS
    assert a_w_qkv.shape[2] == (A_HEADS + 2 * A_KV_HEADS) * A_HEAD_DIM and b_rpb.shape[1] == B_HEADS
    assert c_w_dqkv.shape[2] == C_Q_RANK + C_KV_RANK + C_ROPE and d == B_HEADS * B_HEAD_DIM
    alpha = (2.0 * depth) ** 0.25

    xa = jnp.concatenate([x, ctx], axis=1)
    cond = jnp.zeros((MOD_ROWS, d), F32).at[:bsz].set(c).at[bsz].set(c_ctx)
    mod = _ada_call(cond, w_ada, b_ada)

    cos_a, sin_a = _rope_tables(n_lat, n_ctx, A_HEAD_DIM // 4)
    cos_c, sin_c = _rope_tables(n_lat, n_ctx, C_ROPE // 4)
    perm_a = _rope_perm(A_HEAD_DIM // 4)
    perm_c = _rope_perm(C_ROPE // 4)

    wg_all, wu_all, wd_all = w_ffn_gate.astype(BF16), w_ffn_up.astype(BF16), w_ffn_down.astype(BF16)
    wo_a, wo_b, wo_c = a_w_o.astype(BF16), b_w_o.astype(BF16), c_w_o.astype(BF16)

    ctx_chunks = ((n_lat, n_ctx),)
    lat_chunks = ((0, 2048), (2048, t - 2048))

    for i in range(depth):
        last = i == depth - 1
        out_rows = n_lat if last else t
        kind, j = i % 3, i // 3
        if kind == 0:
            hd = A_HEAD_DIM
            nqk = A_HEADS + A_KV_HEADS
            col_idx = np.concatenate([h * hd + perm_a for h in range(nqk)]
                                     + [np.arange(nqk * hd, (nqk + A_KV_HEADS) * hd)])
            w = a_w_qkv[j][:, col_idx].astype(BF16)
            qkv = _qkv_a_call(xa, mod, i, w, a_q_gain[j][perm_a][None, :], a_k_gain[j][perm_a][None, :],
                              cos_a, sin_a, n_lat)
            o = _flash_call(qkv, qkv, qkv, name="attn_gqa", n_groups=A_KV_HEADS, n_sub=1, n_split=2,
                            n_heads=A_HEADS // A_KV_HEADS, dk=hd, dv=hd, tq=256, k_col0=A_HEADS,
                            v_col0=A_HEADS + A_KV_HEADS, n_lat=n_lat, lat_chunks=lat_chunks,
                            ctx_chunks=ctx_chunks, out_rows=out_rows)
            wo_all = wo_a
        elif kind == 1:
            qkv = _qkv_b_call(xa, mod, i, b_w_qkv[j].astype(BF16), n_lat)
            o = _na_call(qkv, _na_bias_tables(b_rpb[j], rows), n_lat, n_ctx, out_rows)
            wo_all = wo_b
        else:
            r1 = C_Q_RANK + C_KV_RANK
            wd_idx = np.concatenate([np.arange(r1), np.where(perm_c < 0, -1, r1 + perm_c)])
            wdn = _take_cols(c_w_dqkv[j], wd_idx).astype(BF16)
            hq = C_NOPE + C_ROPE
            uq_idx = np.concatenate([np.concatenate([h * hq + np.arange(C_NOPE),
                                                     np.where(perm_c < 0, -1, h * hq + C_NOPE + perm_c)])
                                     for h in range(C_HEADS)])
            wuq = _take_cols(c_w_uq[j], uq_idx).astype(BF16)
            qc, kc, vc = _proj_c_call(xa, mod, i, wdn, c_q_a_gain[j][None, :], c_kv_a_gain[j][None, :], wuq,
                                      c_w_ukv[j].astype(BF16), cos_c, sin_c, n_lat)
            o = _flash_call(qc, kc, vc, name="attn_mla", n_groups=C_HEADS, n_sub=1, n_split=2, n_heads=1, dk=2 * LANES,
                            dv=C_V, tq=1024, k_col0=0, v_col0=0, n_lat=n_lat, lat_chunks=lat_chunks,
                            ctx_chunks=ctx_chunks, out_rows=out_rows)
            wo_all = wo_c
        xa = _post_call(o, xa, mod, i, wo_all, j, wg_all, wu_all, wd_all, ln_g, ln_b, n_lat, out_rows, alpha)
    return xa
```

```python
import functools
import math

import numpy as np
import jax
import jax.numpy as jnp
from jax import lax
from jax.experimental import pallas as pl
from jax.experimental.pallas import tpu as pltpu

GRID_W = 64
A_HEADS, A_KV_HEADS, A_HEAD_DIM = 8, 2, 128
B_HEADS, B_HEAD_DIM, B_MAX_KH, B_KW = 16, 64, 8, 16
C_HEADS, C_NOPE, C_ROPE, C_V, C_Q_RANK, C_KV_RANK = 8, 128, 64, 128, 384, 256
ROPE_BASE = 10000.0
RMS_EPS = 1e-6
LN_EPS = 1e-5
LOG2E = math.log2(math.e)
NEG_BIG = -1e30

LANES = 128
V7X_VMEM_BYTES = 64 * 1024 * 1024
VMEM_LIMIT = 56 * 1024 * 1024

ROW_TILE = 256
ROW_BATCHES = 2
MOD_ROWS = 8
NA_Q_ROWS = 4
NA_K_ROWS = 12
NA_PAIRS_PER_STEP = 4

BF16 = jnp.bfloat16
F32 = jnp.float32


def _cparams(n_axes, vmem=None):
    return pltpu.CompilerParams(dimension_semantics=("arbitrary",) * n_axes, vmem_limit_bytes=vmem)


def _resident(shape, index_map):
    return pl.BlockSpec(shape, index_map, pipeline_mode=pl.Buffered(1))


def _ada_kernel(c_ref, w_ref, b_ref, o_ref):
    c = c_ref[...]
    cs = c / (1.0 + jnp.exp(-c))
    o_ref[...] = jnp.dot(cs, w_ref[...], preferred_element_type=F32) + b_ref[...]


def _ada_call(cond, w_ada, b_ada):
    depth, d, n6 = w_ada.shape
    tn = n6 // 4
    return pl.pallas_call(
        _ada_kernel,
        grid=(depth, n6 // tn),
        in_specs=[
            pl.BlockSpec((MOD_ROWS, d), lambda l, j: (0, 0)),
            pl.BlockSpec((None, d, tn), lambda l, j: (l, 0, j)),
            pl.BlockSpec((None, 1, tn), lambda l, j: (l, 0, j)),
        ],
        out_specs=pl.BlockSpec((None, MOD_ROWS, tn), lambda l, j: (l, 0, j)),
        out_shape=jax.ShapeDtypeStruct((depth, MOD_ROWS, n6), F32),
        compiler_params=_cparams(2, 40 * 1024 * 1024),
        name="ada_mod",
    )(cond, w_ada, b_ada.reshape(depth, 1, n6))


def _mod_rows(nb, n_lat_tiles, ctx_row):
    b, i = pl.program_id(0), pl.program_id(1)
    return [jnp.where(i < n_lat_tiles, b * nb + s, ctx_row) for s in range(nb)]


def _mod_vec(mod_ref, row, k, d):
    return mod_ref[pl.ds(row, 1), k * d:(k + 1) * d]


def _modulated(x_ref, mod_ref, rows, k_shift, d):
    parts = []
    for s, row in enumerate(rows):
        sh = _mod_vec(mod_ref, row, k_shift, d)
        sc = _mod_vec(mod_ref, row, k_shift + 1, d)
        parts.append((x_ref[s] * (1.0 + sc) + sh).astype(BF16))
    return jnp.concatenate(parts, axis=0)


def _store_rows(o_ref, cols, val):
    nb, tm, _ = o_ref.shape
    o_ref[:, :, cols] = val.astype(o_ref.dtype).reshape(nb, tm, val.shape[-1])


def _row_block(nb, tm, w):
    return pl.BlockSpec((nb, tm, w), lambda b, i: (b, i, 0))


def _rms(t, g):
    return t * lax.rsqrt(jnp.mean(t * t, axis=-1, keepdims=True) + RMS_EPS) * g


def _rope(t, cos, sin):
    return t * cos + pltpu.roll(t, LANES // 2, 1) * sin


def _layer_norm(t, g, b):
    mu = jnp.mean(t, axis=-1, keepdims=True)
    c = t - mu
    var = jnp.mean(c * c, axis=-1, keepdims=True)
    return c * lax.rsqrt(var + LN_EPS) * g + b


def _qkv_a_kernel(x_ref, mod_ref, w_ref, gq_ref, gk_ref, cos_ref, sin_ref, o_ref, *, n_lat_tiles, ctx_row, q_scale):
    nb, tm, d = x_ref.shape
    h = _modulated(x_ref, mod_ref, _mod_rows(nb, n_lat_tiles, ctx_row), 0, d)
    cos, sin = cos_ref[...], sin_ref[...]
    gq = gq_ref[...] * q_scale
    gk = gk_ref[...]
    hd = A_HEAD_DIM
    v0 = (A_HEADS + A_KV_HEADS) * hd
    accs = [jnp.dot(h[s * tm:(s + 1) * tm], w_ref[...], preferred_element_type=F32) for s in range(nb)]
    for s, acc in enumerate(accs):
        for j in range(A_HEADS + A_KV_HEADS):
            t = acc[:, j * hd:(j + 1) * hd]
            y = _rope(_rms(t, gq if j < A_HEADS else gk), cos, sin)
            o_ref[s, :, j * hd:(j + 1) * hd] = y.astype(BF16)
        o_ref[s, :, v0:] = acc[:, v0:].astype(BF16)


def _qkv_a_call(xa, mod, layer, w, gq, gk, cos, sin, n_lat):
    bsz, t, d = xa.shape
    n_out = w.shape[1]
    tm = ROW_TILE
    kern = functools.partial(_qkv_a_kernel, n_lat_tiles=n_lat // tm, ctx_row=bsz,
                             q_scale=A_HEAD_DIM ** -0.5 * LOG2E)
    nb = ROW_BATCHES
    return pl.pallas_call(
        kern,
        grid=(bsz // nb, t // tm),
        in_specs=[
            _row_block(nb, tm, d),
            pl.BlockSpec((None, MOD_ROWS, 6 * d), lambda b, i: (layer, 0, 0)),
            _resident((d, n_out), lambda b, i: (0, 0)),
            pl.BlockSpec((1, A_HEAD_DIM), lambda b, i: (0, 0)),
            pl.BlockSpec((1, A_HEAD_DIM), lambda b, i: (0, 0)),
            pl.BlockSpec((tm, LANES), lambda b, i: (i, 0)),
            pl.BlockSpec((tm, LANES), lambda b, i: (i, 0)),
        ],
        out_specs=_row_block(nb, tm, n_out),
        out_shape=jax.ShapeDtypeStruct((bsz, t, n_out), BF16),
        compiler_params=_cparams(2, 32 * 1024 * 1024),
        name="qkv_gqa",
    )(xa, mod, w, gq, gk, cos, sin)


def _qkv_b_kernel(x_ref, mod_ref, w_ref, o_ref, *, n_lat_tiles, ctx_row, q_scale):
    nb, _, d = x_ref.shape
    h = _modulated(x_ref, mod_ref, _mod_rows(nb, n_lat_tiles, ctx_row), 0, d)
    acc = jnp.dot(h, w_ref[...], preferred_element_type=F32)
    _store_rows(o_ref, slice(0, d), acc[:, :d] * q_scale)
    _store_rows(o_ref, slice(d, acc.shape[1]), acc[:, d:])


def _qkv_b_call(xa, mod, layer, w, n_lat):
    bsz, t, d = xa.shape
    n_out = w.shape[1]
    tm = ROW_TILE
    kern = functools.partial(_qkv_b_kernel, n_lat_tiles=n_lat // tm, ctx_row=bsz,
                             q_scale=B_HEAD_DIM ** -0.5 * LOG2E)
    return pl.pallas_call(
        kern,
        grid=(bsz // ROW_BATCHES, t // tm),
        in_specs=[
            _row_block(ROW_BATCHES, tm, d),
            pl.BlockSpec((None, MOD_ROWS, 6 * d), lambda b, i: (layer, 0, 0)),
            _resident((d, n_out), lambda b, i: (0, 0)),
        ],
        out_specs=_row_block(ROW_BATCHES, tm, n_out),
        out_shape=jax.ShapeDtypeStruct((bsz, t, n_out), BF16),
        compiler_params=_cparams(2, 40 * 1024 * 1024),
        name="qkv_na",
    )(xa, mod, w)


def _proj_c_kernel(x_ref, mod_ref, wd_ref, gq_ref, gkv_ref, wuq_ref, wukv_ref, cos_ref, sin_ref,
                   q_ref, k_ref, v_ref, *, n_lat_tiles, ctx_row, q_scale):
    nb, tm, d = x_ref.shape
    h = _modulated(x_ref, mod_ref, _mod_rows(nb, n_lat_tiles, ctx_row), 0, d)
    cos, sin = cos_ref[...], sin_ref[...]
    r0, r1 = C_Q_RANK, C_Q_RANK + C_KV_RANK
    w2 = 2 * LANES
    dns = [jnp.dot(h[s * tm:(s + 1) * tm], wd_ref[...], preferred_element_type=F32) for s in range(nb)]
    qls = [_rms(dn[:, :r0], gq_ref[...]).astype(BF16) for dn in dns]
    kvls = [_rms(dn[:, r0:r1], gkv_ref[...]).astype(BF16) for dn in dns]
    qs = [jnp.dot(ql, wuq_ref[...], preferred_element_type=F32) for ql in qls]
    kvs = [jnp.dot(kvl, wukv_ref[...], preferred_element_type=F32) for kvl in kvls]
    for s in range(nb):
        q, kv = qs[s], kvs[s]
        kr = _rope(dns[s][:, r1:], cos, sin).astype(BF16)
        for hh in range(C_HEADS):
            nope, rot = slice(hh * w2, hh * w2 + LANES), slice(hh * w2 + LANES, (hh + 1) * w2)
            q_ref[s, :, nope] = (q[:, nope] * q_scale).astype(BF16)
            q_ref[s, :, rot] = (_rope(q[:, rot], cos, sin) * q_scale).astype(BF16)
            k_ref[s, :, nope] = kv[:, nope].astype(BF16)
            k_ref[s, :, rot] = kr
            v_ref[s, :, hh * LANES:(hh + 1) * LANES] = kv[:, rot].astype(BF16)


def _proj_c_call(xa, mod, layer, wd, gq, gkv, wuq, wukv, cos, sin, n_lat):
    bsz, t, d = xa.shape
    tm = ROW_TILE
    kern = functools.partial(_proj_c_kernel, n_lat_tiles=n_lat // tm, ctx_row=bsz,
                             q_scale=(C_NOPE + C_ROPE) ** -0.5 * LOG2E)
    const = lambda b, i: (0, 0)
    qk_w = C_HEADS * 2 * LANES
    return pl.pallas_call(
        kern,
        grid=(bsz // ROW_BATCHES, t // tm),
        in_specs=[
            _row_block(ROW_BATCHES, tm, d),
            pl.BlockSpec((None, MOD_ROWS, 6 * d), lambda b, i: (layer, 0, 0)),
            _resident(wd.shape, const),
            pl.BlockSpec(gq.shape, const),
            pl.BlockSpec(gkv.shape, const),
            _resident(wuq.shape, const),
            _resident(wukv.shape, const),
            pl.BlockSpec((tm, LANES), lambda b, i: (i, 0)),
            pl.BlockSpec((tm, LANES), lambda b, i: (i, 0)),
        ],
        out_specs=[_row_block(ROW_BATCHES, tm, qk_w), _row_block(ROW_BATCHES, tm, qk_w),
                   _row_block(ROW_BATCHES, tm, C_HEADS * C_V)],
        out_shape=[
            jax.ShapeDtypeStruct((bsz, t, qk_w), BF16),
            jax.ShapeDtypeStruct((bsz, t, qk_w), BF16),
            jax.ShapeDtypeStruct((bsz, t, C_HEADS * C_V), BF16),
        ],
        compiler_params=_cparams(2, 40 * 1024 * 1024),
        name="proj_mla",
    )(xa, mod, wd, gq, gkv, wuq, wukv, cos, sin)


def _pair_masks(shape):
    lane = lax.broadcasted_iota(jnp.int32, shape, 1)
    return lane < (LANES // 2)


def _dot_nt(a, b):
    return lax.dot_general(a, b, (((1,), (1,)), ((), ())), preferred_element_type=F32)


def _flash_kernel(q_ref, k_ref, v_ref, o_ref, *, n_sub, n_split, n_heads, dk, dv, tq_lat, lat_chunks,
                  ctx_rows, ctx_chunks):
    def run(row0, tq, chunks):
        m_rows = n_heads * tq // n_split
        rows = pl.ds(row0, tq)
        chains, qs, state = [], [], []
        for u in range(n_sub):
            q0 = u * n_heads * dk
            q = jnp.concatenate([q_ref[rows, q0 + r * dk:q0 + (r + 1) * dk] for r in range(n_heads)], axis=0)
            for h in range(n_split):
                chains.append(u)
                qs.append(q[h * m_rows:(h + 1) * m_rows])
                state.append((jnp.full((m_rows, 1), NEG_BIG, F32), jnp.zeros((m_rows, dv + LANES), F32)))
        for start, size in chunks:
            ones = jnp.ones((size, LANES), BF16)
            scores = [_dot_nt(qs[c], k_ref[start:start + size, u * dk:(u + 1) * dk])
                      for c, u in enumerate(chains)]
            for c, u in enumerate(chains):
                m, acc = state[c]
                v1 = jnp.concatenate([v_ref[start:start + size, u * dv:(u + 1) * dv], ones], axis=1)
                m_new = jnp.maximum(m, jnp.max(scores[c], axis=-1, keepdims=True))
                alpha = jnp.exp2(m - m_new)
                p = jnp.exp2(scores[c] - m_new)
                acc = alpha * acc + jnp.dot(p.astype(BF16), v1, preferred_element_type=F32)
                state[c] = (m_new, acc)
        for u in range(n_sub):
            accs = [state[u * n_split + h][1] for h in range(n_split)]
            acc = accs[0] if n_split == 1 else jnp.concatenate(accs, axis=0)
            o = acc[:, :dv] / acc[:, dv:]
            o0 = u * n_heads * dv
            for r in range(n_heads):
                o_ref[rows, o0 + r * dv:o0 + (r + 1) * dv] = o[r * tq:(r + 1) * tq].astype(o_ref.dtype)

    n_ctx_steps = 0 if ctx_rows is None else 1
    i = pl.program_id(2) - n_ctx_steps

    @pl.when(i >= 0)
    def _():
        run(pl.multiple_of(i * tq_lat, tq_lat), tq_lat, lat_chunks)

    if ctx_rows is not None:
        @pl.when(i < 0)
        def _():
            run(ctx_rows[0], ctx_rows[1], ctx_chunks)


def _flash_call(q_arr, k_arr, v_arr, *, name, n_groups, n_sub, n_split, n_heads, dk, dv, tq, k_col0, v_col0, n_lat,
                lat_chunks, ctx_chunks, out_rows):
    bsz, t, _ = k_arr.shape
    assert n_groups % n_sub == 0 and k_col0 % n_sub == 0 and v_col0 % n_sub == 0 and n_lat % tq == 0
    with_ctx = out_rows > n_lat
    kern = functools.partial(_flash_kernel, n_sub=n_sub, n_split=n_split, n_heads=n_heads, dk=dk, dv=dv, tq_lat=tq,
                             lat_chunks=lat_chunks,
                             ctx_rows=(n_lat, out_rows - n_lat) if with_ctx else None, ctx_chunks=ctx_chunks)
    return pl.pallas_call(
        kern,
        grid=(bsz, n_groups // n_sub, n_lat // tq + int(with_ctx)),
        in_specs=[
            pl.BlockSpec((None, t, n_sub * n_heads * dk), lambda b, g, i: (b, 0, g)),
            pl.BlockSpec((None, t, n_sub * dk), lambda b, g, i: (b, 0, k_col0 // n_sub + g)),
            pl.BlockSpec((None, t, n_sub * dv), lambda b, g, i: (b, 0, v_col0 // n_sub + g)),
        ],
        out_specs=pl.BlockSpec((None, out_rows, n_sub * n_heads * dv), lambda b, g, i: (b, 0, g)),
        out_shape=jax.ShapeDtypeStruct((bsz, out_rows, n_groups * n_heads * dv), BF16),
        compiler_params=_cparams(3, VMEM_LIMIT),
        name=name,
    )(q_arr, k_arr, v_arr)


def _na_kernel(q_ref, k_ref, v_ref, bias_ref, o_ref, *, n_lat, n_ctx, n_groups, n_ctx_steps):
    gi = pl.program_id(2) - n_ctx_steps
    rows = n_groups * NA_Q_ROWS
    tq = NA_Q_ROWS * GRID_W
    tk = NA_K_ROWS * GRID_W
    lo = _pair_masks((tq, LANES))

    def stacked_q(cols):
        qb = q_ref[:, cols]
        zero = jnp.zeros_like(qb)
        return jnp.concatenate([jnp.where(lo, qb, zero), jnp.where(lo, zero, qb)], axis=0)

    def with_ones(v):
        return jnp.concatenate([v, jnp.ones(v.shape, v.dtype)], axis=1)

    def finish(o1, cols):
        o = o1[:, :LANES] / o1[:, LANES:]
        o_ref[:, cols] = jnp.where(lo, o[:tq], o[tq:]).astype(o_ref.dtype)

    @pl.when(gi >= 0)
    def _():
        start = jnp.clip(gi * NA_Q_ROWS - B_MAX_KH // 2, 0, rows - NA_K_ROWS) * GRID_W
        start = pl.multiple_of(start, GRID_W)
        scores = []
        for s in range(NA_PAIRS_PER_STEP):
            cols = slice(s * LANES, (s + 1) * LANES)
            kk = jnp.concatenate([k_ref[pl.ds(start, tk), cols], k_ref[n_lat:n_lat + n_ctx, cols]], axis=0)
            scores.append(_dot_nt(stacked_q(cols), kk))
        for s in range(NA_PAIRS_PER_STEP):
            cols = slice(s * LANES, (s + 1) * LANES)
            vv = with_ones(jnp.concatenate([v_ref[pl.ds(start, tk), cols], v_ref[n_lat:n_lat + n_ctx, cols]], axis=0))
            bias = bias_ref[2 * s:2 * s + 2].reshape(2 * tq, tk)
            sc = jnp.concatenate([scores[s][:, :tk] + bias, scores[s][:, tk:]], axis=1)
            p = jnp.exp2(sc - jnp.max(sc, axis=-1, keepdims=True))
            finish(jnp.dot(p.astype(BF16), vv, preferred_element_type=F32), cols)

    @pl.when(gi < 0)
    def _():
        for s in range(NA_PAIRS_PER_STEP):
            cols = slice(s * LANES, (s + 1) * LANES)
            q = stacked_q(cols)
            s_c = _dot_nt(q, k_ref[n_lat:n_lat + n_ctx, cols])
            p_c = jnp.exp2(s_c - jnp.max(s_c, axis=-1, keepdims=True))
            vc = with_ones(v_ref[n_lat:n_lat + n_ctx, cols])
            finish(jnp.dot(p_c.astype(BF16), vc, preferred_element_type=F32), cols)


def _na_call(qkv, bias, n_lat, n_ctx, out_rows):
    bsz, t, w3 = qkv.shape
    d = w3 // 3
    w = NA_PAIRS_PER_STEP * LANES
    n_steps = d // w
    tq = NA_Q_ROWS * GRID_W
    tk = NA_K_ROWS * GRID_W
    n_groups = n_lat // tq
    n_ctx_steps = out_rows // tq - n_groups
    kern = functools.partial(_na_kernel, n_lat=n_lat, n_ctx=n_ctx, n_groups=n_groups, n_ctx_steps=n_ctx_steps)

    def row_tile(g):
        return jnp.where(g < n_ctx_steps, n_groups, g - n_ctx_steps)

    def bias_map(b, p, g):
        gi = g - n_ctx_steps
        case = jnp.where(gi <= 0, 0, jnp.where(gi == n_groups - 1, 2, 1))
        return (p, case, 0, 0)

    assert n_ctx == tq and n_ctx_steps in (0, 1)
    return pl.pallas_call(
        kern,
        grid=(bsz, n_steps, out_rows // tq),
        in_specs=[
            pl.BlockSpec((None, tq, w), lambda b, p, g: (b, row_tile(g), p)),
            pl.BlockSpec((None, t, w), lambda b, p, g: (b, 0, n_steps + p)),
            pl.BlockSpec((None, t, w), lambda b, p, g: (b, 0, 2 * n_steps + p)),
            pl.BlockSpec((2 * NA_PAIRS_PER_STEP, None, tq, tk), bias_map),
        ],
        out_specs=pl.BlockSpec((None, tq, w), lambda b, p, g: (b, row_tile(g), p)),
        out_shape=jax.ShapeDtypeStruct((bsz, out_rows, d), BF16),
        compiler_params=_cparams(3, 48 * 1024 * 1024),
        name="attn_na",
    )(qkv, qkv, qkv, bias)


def _post_kernel(o_ref, x_ref, mod_ref, wo_ref, wg_ref, wu_ref, wd_ref, lng_ref, lnb_ref, out_ref,
                 *, n_lat_tiles, ctx_row, alpha):
    nb, tm, d = x_ref.shape
    rows = _mod_rows(nb, n_lat_tiles, ctx_row)
    ys = [jnp.dot(o_ref[s], wo_ref[...], preferred_element_type=F32) for s in range(nb)]
    x1s, hs = [], []
    for s, row in enumerate(rows):
        g1 = _mod_vec(mod_ref, row, 2, d)
        sh2 = _mod_vec(mod_ref, row, 3, d)
        sc2 = _mod_vec(mod_ref, row, 4, d)
        x1 = _layer_norm(alpha * x_ref[s] + g1 * ys[s], lng_ref[0:1, :], lnb_ref[0:1, :])
        x1s.append(x1)
        hs.append((x1 * (1.0 + sc2) + sh2).astype(BF16))
    gates = [jnp.dot(h, wg_ref[...], preferred_element_type=F32) for h in hs]
    ups = [jnp.dot(h, wu_ref[...], preferred_element_type=F32) for h in hs]
    acts = [(g / (1.0 + jnp.exp(-g)) * u).astype(BF16) for g, u in zip(gates, ups)]
    fs = [jnp.dot(a, wd_ref[...], preferred_element_type=F32) for a in acts]
    for s, row in enumerate(rows):
        g2 = _mod_vec(mod_ref, row, 5, d)
        out_ref[s] = _layer_norm(alpha * x1s[s] + g2 * fs[s], lng_ref[1:2, :], lnb_ref[1:2, :])


def _post_call(o, xa, mod, layer, wo_all, wo_idx, wg_all, wu_all, wd_all, lng, lnb, n_lat, out_rows, alpha):
    bsz, _, d = xa.shape
    tm = ROW_TILE
    d_ff = wg_all.shape[2]
    kern = functools.partial(_post_kernel, n_lat_tiles=n_lat // tm, ctx_row=bsz, alpha=alpha)
    return pl.pallas_call(
        kern,
        grid=(bsz // ROW_BATCHES, out_rows // tm),
        in_specs=[
            _row_block(ROW_BATCHES, tm, o.shape[-1]),
            _row_block(ROW_BATCHES, tm, d),
            pl.BlockSpec((None, MOD_ROWS, 6 * d), lambda b, i: (layer, 0, 0)),
            _resident((None,) + wo_all.shape[1:], lambda b, i: (wo_idx, 0, 0)),
            _resident((None, d, d_ff), lambda b, i: (layer, 0, 0)),
            _resident((None, d, d_ff), lambda b, i: (layer, 0, 0)),
            _resident((None, d_ff, d), lambda b, i: (layer, 0, 0)),
            pl.BlockSpec((None, 2, d), lambda b, i: (layer, 0, 0)),
            pl.BlockSpec((None, 2, d), lambda b, i: (layer, 0, 0)),
        ],
        out_specs=_row_block(ROW_BATCHES, tm, d),
        out_shape=jax.ShapeDtypeStruct((bsz, out_rows, d), F32),
        compiler_params=_cparams(2, VMEM_LIMIT),
        name="post_ffn",
    )(o, xa, mod, wo_all, wg_all, wu_all, wd_all, lng, lnb)


def _rope_tables(n_lat, n_ctx, n_freq):
    half = LANES // 2
    t = np.arange(n_lat)
    freqs = ROPE_BASE ** (-np.arange(0, 2 * n_freq, 2, dtype=np.float64) / (2 * n_freq))
    ang = np.concatenate([(t // GRID_W)[:, None] * freqs[None, :], (t % GRID_W)[:, None] * freqs[None, :]], axis=1)
    cos = np.zeros((n_lat + n_ctx, LANES))
    sin = np.zeros((n_lat + n_ctx, LANES))
    w = 2 * n_freq
    cos[:n_lat, :w] = np.cos(ang)
    cos[:n_lat, half:half + w] = np.cos(ang)
    sin[:n_lat, :w] = -np.sin(ang)
    sin[:n_lat, half:half + w] = np.sin(ang)
    cos[n_lat:, :w] = 1.0
    cos[n_lat:, half:half + w] = 1.0
    return jnp.asarray(cos, F32), jnp.asarray(sin, F32)


def _rope_perm(n_freq):
    idx = np.full((LANES,), -1, np.int64)
    f = n_freq
    idx[0:f] = np.arange(0, f)
    idx[f:2 * f] = np.arange(2 * f, 3 * f)
    idx[64:64 + f] = np.arange(f, 2 * f)
    idx[64 + f:64 + 2 * f] = np.arange(3 * f, 4 * f)
    return idx


def _take_cols(w, idx):
    wz = jnp.concatenate([w, jnp.zeros((w.shape[0], 1), w.dtype)], axis=1)
    return wz[:, np.where(idx < 0, w.shape[1], idx)]


def _na_row_cases(rows):
    kh = B_MAX_KH
    i = np.arange(NA_Q_ROWS)[:, None]
    j = np.arange(NA_K_ROWS)[None, :]
    cases = []
    for r0 in (0, NA_Q_ROWS, rows - NA_Q_ROWS):
        start = int(np.clip(r0 - kh // 2, 0, rows - NA_K_ROWS))
        rs = np.clip(r0 + i - kh // 2, 0, rows - kh)
        key_row = start + j
        cases.append(((key_row >= rs) & (key_row < rs + kh), key_row - (r0 + i) + (kh - 1)))
    return np.stack([c[0] for c in cases]), np.stack([c[1] for c in cases])


def _na_bias_kernel(r2_ref, o_ref, *, row_ok, dr_idx):
    n_e = r2_ref.shape[0]
    shape = (GRID_W, LANES)
    c = lax.broadcasted_iota(jnp.int32, shape, 0)
    lane = lax.broadcasted_iota(jnp.int32, shape, 1)
    kc = jnp.where(lane < GRID_W, lane, lane - GRID_W)
    cs = jnp.clip(c - B_KW // 2, 0, GRID_W - B_KW)
    col_ok = (kc >= cs) & (kc < cs + B_KW)
    left = lane < GRID_W
    neg = jnp.full(shape, NEG_BIG, F32)
    x = r2_ref[...]
    tiles = []
    for e in range(n_e):
        t = pltpu.roll(jnp.broadcast_to(x[e:e + 1, :], shape), LANES - (B_KW - 1), 1, stride=1, stride_axis=0)
        tiles.append(jnp.where(col_ok, t * LOG2E, neg))
    for case in range(row_ok.shape[0]):
        for i in range(NA_Q_ROWS):
            for jp in range(NA_K_ROWS // 2):
                ok_l, ok_r = bool(row_ok[case, i, 2 * jp]), bool(row_ok[case, i, 2 * jp + 1])
                e = int(np.clip(dr_idx[case, i, 2 * jp] + 1, 0, n_e - 1))
                if ok_l and ok_r:
                    tile = tiles[e]
                elif ok_l:
                    tile = jnp.where(left, tiles[e], neg)
                elif ok_r:
                    tile = jnp.where(left, neg, tiles[e])
                else:
                    tile = neg
                o_ref[case, i * GRID_W:(i + 1) * GRID_W, jp * LANES:(jp + 1) * LANES] = tile


def _na_bias_tables(rpb, rows):
    n_heads, n_dr, n_dc = rpb.shape
    n_e = n_dr + 1
    r2 = jnp.zeros((n_heads, n_e, LANES), F32)
    r2 = r2.at[:, 1:, :n_dc].set(rpb).at[:, :n_dr, GRID_W:GRID_W + n_dc].set(rpb)
    row_ok, dr_idx = _na_row_cases(rows)
    kern = functools.partial(_na_bias_kernel, row_ok=row_ok, dr_idx=dr_idx)
    tq, tk = NA_Q_ROWS * GRID_W, NA_K_ROWS * GRID_W
    return pl.pallas_call(
        kern,
        grid=(n_heads,),
        in_specs=[pl.BlockSpec((None, n_e, LANES), lambda h: (h, 0, 0))],
        out_specs=pl.BlockSpec((None, 3, tq, tk), lambda h: (h, 0, 0, 0)),
        out_shape=jax.ShapeDtypeStruct((n_heads, 3, tq, tk), F32),
        compiler_params=_cparams(1, 32 * 1024 * 1024),
        name="na_bias",
    )(r2)


def kernel(x, c, ctx, c_ctx, w_ada, b_ada, ln_g, ln_b, w_ffn_gate, w_ffn_up, w_ffn_down, a_w_qkv, a_q_gain, a_k_gain, a_w_o, b_w_qkv, b_rpb, b_w_o, c_w_dqkv, c_q_a_gain, c_kv_a_gain, c_w_uq, c_w_ukv, c_w_o):
    bsz, n_lat, d = x.shape
    n_ctx = ctx.shape[1]
    t = n_lat + n_ctx
    depth = w_ada.shape[0]
    rows = n_lat // GRID_W
    assert n_lat % ROW_TILE == 0 and n_ctx == ROW_TILE and bsz < MOD_ROWS and bsz % ROW_BATCHES == 0
    assert n_lat % (NA_Q_ROWS * GRID_W) == 0 and rows >= NA_K_ROWS
    assert a_w_qkv.shape[2] == (A_HEADS + 2 * A_KV_HEADS) * A_HEAD_DIM and b_rpb.shape[1] == B_HEADS
    assert c_w_dqkv.shape[2] == C_Q_RANK + C_KV_RANK + C_ROPE and d == B_HEADS * B_HEAD_DIM
    alpha = (2.0 * depth) ** 0.25

    xa = jnp.concatenate([x, ctx], axis=1)
    cond = jnp.zeros((MOD_ROWS, d), F32).at[:bsz].set(c).at[bsz].set(c_ctx)
    mod = _ada_call(cond, w_ada, b_ada)

    cos_a, sin_a = _rope_tables(n_lat, n_ctx, A_HEAD_DIM // 4)
    cos_c, sin_c = _rope_tables(n_lat, n_ctx, C_ROPE // 4)
    perm_a = _rope_perm(A_HEAD_DIM // 4)
    perm_c = _rope_perm(C_ROPE // 4)

    wg_all, wu_all, wd_all = w_ffn_gate.astype(BF16), w_ffn_up.astype(BF16), w_ffn_down.astype(BF16)
    wo_a, wo_b, wo_c = a_w_o.astype(BF16), b_w_o.astype(BF16), c_w_o.astype(BF16)

    ctx_chunks = ((n_lat, n_ctx),)
    lat_chunks = ((0, 2048), (2048, t - 2048))

    for i in range(depth):
        last = i == depth - 1
        out_rows = n_lat if last else t
        kind, j = i % 3, i // 3
        if kind == 0:
            hd = A_HEAD_DIM
            nqk = A_HEADS + A_KV_HEADS
            col_idx = np.concatenate([h * hd + perm_a for h in range(nqk)]
                                     + [np.arange(nqk * hd, (nqk + A_KV_HEADS) * hd)])
            w = a_w_qkv[j][:, col_idx].astype(BF16)
            qkv = _qkv_a_call(xa, mod, i, w, a_q_gain[j][perm_a][None, :], a_k_gain[j][perm_a][None, :],
                              cos_a, sin_a, n_lat)
            o = _flash_call(qkv, qkv, qkv, name="attn_gqa", n_groups=A_KV_HEADS, n_sub=1, n_split=2,
                            n_heads=A_HEADS // A_KV_HEADS, dk=hd, dv=hd, tq=256, k_col0=A_HEADS,
                            v_col0=A_HEADS + A_KV_HEADS, n_lat=n_lat, lat_chunks=lat_chunks,
                            ctx_chunks=ctx_chunks, out_rows=out_rows)
            wo_all = wo_a
        elif kind == 1:
            qkv = _qkv_b_call(xa, mod, i, b_w_qkv[j].astype(BF16), n_lat)
            o = _na_call(qkv, _na_bias_tables(b_rpb[j], rows), n_lat, n_ctx, out_rows)
            wo_all = wo_b
        else:
            r1 = C_Q_RANK + C_KV_RANK
            wd_idx = np.concatenate([np.arange(r1), np.where(perm_c < 0, -1, r1 + perm_c)])
            wdn = _take_cols(c_w_dqkv[j], wd_idx).astype(BF16)
            hq = C_NOPE + C_ROPE
            uq_idx = np.concatenate([np.concatenate([h * hq + np.arange(C_NOPE),
                                                     np.where(perm_c < 0, -1, h * hq + C_NOPE + perm_c)])
                                     for h in range(C_HEADS)])
            wuq = _take_cols(c_w_uq[j], uq_idx).astype(BF16)
            qc, kc, vc = _proj_c_call(xa, mod, i, wdn, c_q_a_gain[j][None, :], c_kv_a_gain[j][None, :], wuq,
                                      c_w_ukv[j].astype(BF16), cos_c, sin_c, n_lat)
            o = _flash_call(qc, kc, vc, name="attn_mla", n_groups=C_HEADS, n_sub=1, n_split=2, n_heads=1, dk=2 * LANES,
                            dv=C_V, tq=1024, k_col0=0, v_col0=0, n_lat=n_lat, lat_chunks=lat_chunks,
                            ctx_chunks=ctx_chunks, out_rows=out_rows)
            wo_all = wo_c
        xa = _post_call(o, xa, mod, i, wo_all, j, wg_all, wu_all, wd_all, ln_g, ln_b, n_lat, out_rows, alpha)
    return xa
```

```python
import functools
import math

import numpy as np
import jax
import jax.numpy as jnp
from jax import lax
from jax.experimental import pallas as pl
from jax.experimental.pallas import tpu as pltpu

GRID_W = 64
A_HEADS, A_KV_HEADS, A_HEAD_DIM = 8, 2, 128
B_HEADS, B_HEAD_DIM, B_MAX_KH, B_KW = 16, 64, 8, 16
C_HEADS, C_NOPE, C_ROPE, C_V, C_Q_RANK, C_KV_RANK = 8, 128, 64, 128, 384, 256
ROPE_BASE = 10000.0
RMS_EPS = 1e-6
LN_EPS = 1e-5
LOG2E = math.log2(math.e)
NEG_BIG = -1e30

LANES = 128
V7X_VMEM_BYTES = 64 * 1024 * 1024
VMEM_LIMIT = 56 * 1024 * 1024

ROW_TILE = 256
ROW_BATCHES = 2
MOD_ROWS = 8
NA_Q_ROWS = 4
NA_K_ROWS = 12
NA_PAIRS_PER_STEP = 4

BF16 = jnp.bfloat16
F32 = jnp.float32


def _cparams(n_axes, vmem=None):
    return pltpu.CompilerParams(dimension_semantics=("arbitrary",) * n_axes, vmem_limit_bytes=vmem)


def _resident(shape, index_map):
    return pl.BlockSpec(shape, index_map, pipeline_mode=pl.Buffered(1))


def _ada_kernel(c_ref, w_ref, b_ref, o_ref):
    c = c_ref[...]
    cs = c / (1.0 + jnp.exp(-c))
    o_ref[...] = jnp.dot(cs, w_ref[...], preferred_element_type=F32) + b_ref[...]


def _ada_call(cond, w_ada, b_ada):
    depth, d, n6 = w_ada.shape
    tn = n6 // 4
    return pl.pallas_call(
        _ada_kernel,
        grid=(depth, n6 // tn),
        in_specs=[
            pl.BlockSpec((MOD_ROWS, d), lambda l, j: (0, 0)),
            pl.BlockSpec((None, d, tn), lambda l, j: (l, 0, j)),
            pl.BlockSpec((None, 1, tn), lambda l, j: (l, 0, j)),
        ],
        out_specs=pl.BlockSpec((None, MOD_ROWS, tn), lambda l, j: (l, 0, j)),
        out_shape=jax.ShapeDtypeStruct((depth, MOD_ROWS, n6), F32),
        compiler_params=_cparams(2, 40 * 1024 * 1024),
        name="ada_mod",
    )(cond, w_ada, b_ada.reshape(depth, 1, n6))


def _mod_rows(nb, n_lat_tiles, ctx_row):
    b, i = pl.program_id(0), pl.program_id(1)
    return [jnp.where(i < n_lat_tiles, b * nb + s, ctx_row) for s in range(nb)]


def _mod_vec(mod_ref, row, k, d):
    return mod_ref[pl.ds(row, 1), k * d:(k + 1) * d]


class _Rows:
    def __init__(self, refs, n_lat_tiles):
        self.refs = refs
        self.is_lat = pl.program_id(1) < n_lat_tiles

    def __getitem__(self, s):
        if len(self.refs) == 1:
            return self.refs[0][s]
        return jnp.where(self.is_lat, self.refs[0][s], self.refs[1][s])


def _x_specs(nb, tm, d, n_lat_tiles, split):
    if not split:
        return [_row_block(nb, tm, d)]
    return [pl.BlockSpec((nb, tm, d), lambda b, i: (b, jnp.minimum(i, n_lat_tiles - 1), 0)),
            pl.BlockSpec((nb, tm, d), lambda b, i: (b, 0, 0))]


def _modulated(x_rows, mod_ref, rows, k_shift, d):
    parts = []
    for s, row in enumerate(rows):
        sh = _mod_vec(mod_ref, row, k_shift, d)
        sc = _mod_vec(mod_ref, row, k_shift + 1, d)
        parts.append((x_rows[s] * (1.0 + sc) + sh).astype(BF16))
    return jnp.concatenate(parts, axis=0)


def _store_rows(o_ref, cols, val):
    nb, tm, _ = o_ref.shape
    o_ref[:, :, cols] = val.astype(o_ref.dtype).reshape(nb, tm, val.shape[-1])


def _row_block(nb, tm, w):
    return pl.BlockSpec((nb, tm, w), lambda b, i: (b, i, 0))


def _rms(t, g):
    return t * lax.rsqrt(jnp.mean(t * t, axis=-1, keepdims=True) + RMS_EPS) * g


def _rope(t, cos, sin):
    return t * cos + pltpu.roll(t, LANES // 2, 1) * sin


def _layer_norm(t, g, b):
    mu = jnp.mean(t, axis=-1, keepdims=True)
    c = t - mu
    var = jnp.mean(c * c, axis=-1, keepdims=True)
    return c * lax.rsqrt(var + LN_EPS) * g + b


def _qkv_a_kernel(*refs, n_x, n_lat_tiles, ctx_row, q_scale):
    mod_ref, w_ref, gq_ref, gk_ref, cos_ref, sin_ref, o_ref = refs[n_x:]
    nb, tm, d = refs[0].shape
    h = _modulated(_Rows(refs[:n_x], n_lat_tiles), mod_ref, _mod_rows(nb, n_lat_tiles, ctx_row), 0, d)
    cos, sin = cos_ref[...], sin_ref[...]
    gq = gq_ref[...] * q_scale
    gk = gk_ref[...]
    hd = A_HEAD_DIM
    v0 = (A_HEADS + A_KV_HEADS) * hd
    accs = [jnp.dot(h[s * tm:(s + 1) * tm], w_ref[...], preferred_element_type=F32) for s in range(nb)]
    for s, acc in enumerate(accs):
        for j in range(A_HEADS + A_KV_HEADS):
            t = acc[:, j * hd:(j + 1) * hd]
            y = _rope(_rms(t, gq if j < A_HEADS else gk), cos, sin)
            o_ref[s, :, j * hd:(j + 1) * hd] = y.astype(BF16)
        o_ref[s, :, v0:] = acc[:, v0:].astype(BF16)


def _qkv_a_call(xs, mod, layer, w, gq, gk, cos, sin, n_lat, t):
    bsz, _, d = xs[0].shape
    n_out = w.shape[1]
    tm = ROW_TILE
    kern = functools.partial(_qkv_a_kernel, n_x=len(xs), n_lat_tiles=n_lat // tm, ctx_row=bsz,
                             q_scale=A_HEAD_DIM ** -0.5 * LOG2E)
    nb = ROW_BATCHES
    return pl.pallas_call(
        kern,
        grid=(bsz // nb, t // tm),
        in_specs=_x_specs(nb, tm, d, n_lat // tm, len(xs) == 2) + [
            pl.BlockSpec((None, MOD_ROWS, 6 * d), lambda b, i: (layer, 0, 0)),
            _resident((d, n_out), lambda b, i: (0, 0)),
            pl.BlockSpec((1, A_HEAD_DIM), lambda b, i: (0, 0)),
            pl.BlockSpec((1, A_HEAD_DIM), lambda b, i: (0, 0)),
            pl.BlockSpec((tm, LANES), lambda b, i: (i, 0)),
            pl.BlockSpec((tm, LANES), lambda b, i: (i, 0)),
        ],
        out_specs=_row_block(nb, tm, n_out),
        out_shape=jax.ShapeDtypeStruct((bsz, t, n_out), BF16),
        compiler_params=_cparams(2, 32 * 1024 * 1024),
        name="qkv_gqa",
    )(*xs, mod, w, gq, gk, cos, sin)


def _qkv_b_kernel(x_ref, mod_ref, w_ref, o_ref, *, n_lat_tiles, ctx_row, q_scale):
    nb, _, d = x_ref.shape
    h = _modulated(x_ref, mod_ref, _mod_rows(nb, n_lat_tiles, ctx_row), 0, d)
    acc = jnp.dot(h, w_ref[...], preferred_element_type=F32)
    _store_rows(o_ref, slice(0, d), acc[:, :d] * q_scale)
    _store_rows(o_ref, slice(d, acc.shape[1]), acc[:, d:])


def _qkv_b_call(xa, mod, layer, w, n_lat):
    bsz, t, d = xa.shape
    n_out = w.shape[1]
    tm = ROW_TILE
    kern = functools.partial(_qkv_b_kernel, n_lat_tiles=n_lat // tm, ctx_row=bsz,
                             q_scale=B_HEAD_DIM ** -0.5 * LOG2E)
    return pl.pallas_call(
        kern,
        grid=(bsz // ROW_BATCHES, t // tm),
        in_specs=[
            _row_block(ROW_BATCHES, tm, d),
            pl.BlockSpec((None, MOD_ROWS, 6 * d), lambda b, i: (layer, 0, 0)),
            _resident((d, n_out), lambda b, i: (0, 0)),
        ],
        out_specs=_row_block(ROW_BATCHES, tm, n_out),
        out_shape=jax.ShapeDtypeStruct((bsz, t, n_out), BF16),
        compiler_params=_cparams(2, 40 * 1024 * 1024),
        name="qkv_na",
    )(xa, mod, w)


def _proj_c_kernel(x_ref, mod_ref, wd_ref, gq_ref, gkv_ref, wuq_ref, wukv_ref, cos_ref, sin_ref,
                   q_ref, k_ref, v_ref, *, n_lat_tiles, ctx_row, q_scale):
    nb, tm, d = x_ref.shape
    h = _modulated(x_ref, mod_ref, _mod_rows(nb, n_lat_tiles, ctx_row), 0, d)
    cos, sin = cos_ref[...], sin_ref[...]
    r0, r1 = C_Q_RANK, C_Q_RANK + C_KV_RANK
    w2 = 2 * LANES
    dns = [jnp.dot(h[s * tm:(s + 1) * tm], wd_ref[...], preferred_element_type=F32) for s in range(nb)]
    qls = [_rms(dn[:, :r0], gq_ref[...]).astype(BF16) for dn in dns]
    kvls = [_rms(dn[:, r0:r1], gkv_ref[...]).astype(BF16) for dn in dns]
    qs = [jnp.dot(ql, wuq_ref[...], preferred_element_type=F32) for ql in qls]
    kvs = [jnp.dot(kvl, wukv_ref[...], preferred_element_type=F32) for kvl in kvls]
    for s in range(nb):
        q, kv = qs[s], kvs[s]
        kr = _rope(dns[s][:, r1:], cos, sin).astype(BF16)
        for hh in range(C_HEADS):
            nope, rot = slice(hh * w2, hh * w2 + LANES), slice(hh * w2 + LANES, (hh + 1) * w2)
            q_ref[s, :, nope] = (q[:, nope] * q_scale).astype(BF16)
            q_ref[s, :, rot] = (_rope(q[:, rot], cos, sin) * q_scale).astype(BF16)
            k_ref[s, :, nope] = kv[:, nope].astype(BF16)
            k_ref[s, :, rot] = kr
            v_ref[s, :, hh * LANES:(hh + 1) * LANES] = kv[:, rot].astype(BF16)


def _proj_c_call(xa, mod, layer, wd, gq, gkv, wuq, wukv, cos, sin, n_lat):
    bsz, t, d = xa.shape
    tm = ROW_TILE
    kern = functools.partial(_proj_c_kernel, n_lat_tiles=n_lat // tm, ctx_row=bsz,
                             q_scale=(C_NOPE + C_ROPE) ** -0.5 * LOG2E)
    const = lambda b, i: (0, 0)
    qk_w = C_HEADS * 2 * LANES
    return pl.pallas_call(
        kern,
        grid=(bsz // ROW_BATCHES, t // tm),
        in_specs=[
            _row_block(ROW_BATCHES, tm, d),
            pl.BlockSpec((None, MOD_ROWS, 6 * d), lambda b, i: (layer, 0, 0)),
            _resident(wd.shape, const),
            pl.BlockSpec(gq.shape, const),
            pl.BlockSpec(gkv.shape, const),
            _resident(wuq.shape, const),
            _resident(wukv.shape, const),
            pl.BlockSpec((tm, LANES), lambda b, i: (i, 0)),
            pl.BlockSpec((tm, LANES), lambda b, i: (i, 0)),
        ],
        out_specs=[_row_block(ROW_BATCHES, tm, qk_w), _row_block(ROW_BATCHES, tm, qk_w),
                   _row_block(ROW_BATCHES, tm, C_HEADS * C_V)],
        out_shape=[
            jax.ShapeDtypeStruct((bsz, t, qk_w), BF16),
            jax.ShapeDtypeStruct((bsz, t, qk_w), BF16),
            jax.ShapeDtypeStruct((bsz, t, C_HEADS * C_V), BF16),
        ],
        compiler_params=_cparams(2, 40 * 1024 * 1024),
        name="proj_mla",
    )(xa, mod, wd, gq, gkv, wuq, wukv, cos, sin)


def _pair_masks(shape):
    lane = lax.broadcasted_iota(jnp.int32, shape, 1)
    return lane < (LANES // 2)


def _dot_nt(a, b):
    return lax.dot_general(a, b, (((1,), (1,)), ((), ())), preferred_element_type=F32)


def _flash_kernel(q_ref, k_ref, v_ref, o_ref, *, n_sub, n_split, n_heads, dk, dv, tq_lat, lat_chunks,
                  ctx_rows, ctx_chunks):
    def run(row0, tq, chunks):
        m_rows = n_heads * tq // n_split
        rows = pl.ds(row0, tq)
        chains, qs, state = [], [], []
        for u in range(n_sub):
            q0 = u * n_heads * dk
            q = jnp.concatenate([q_ref[rows, q0 + r * dk:q0 + (r + 1) * dk] for r in range(n_heads)], axis=0)
            for h in range(n_split):
                chains.append(u)
                qs.append(q[h * m_rows:(h + 1) * m_rows])
                state.append((jnp.full((m_rows, 1), NEG_BIG, F32), jnp.zeros((m_rows, dv + LANES), F32)))
        for start, size in chunks:
            ones = jnp.ones((size, LANES), BF16)
            scores = [_dot_nt(qs[c], k_ref[start:start + size, u * dk:(u + 1) * dk])
                      for c, u in enumerate(chains)]
            for c, u in enumerate(chains):
                m, acc = state[c]
                v1 = jnp.concatenate([v_ref[start:start + size, u * dv:(u + 1) * dv], ones], axis=1)
                m_new = jnp.maximum(m, jnp.max(scores[c], axis=-1, keepdims=True))
                alpha = jnp.exp2(m - m_new)
                p = jnp.exp2(scores[c] - m_new)
                acc = alpha * acc + jnp.dot(p.astype(BF16), v1, preferred_element_type=F32)
                state[c] = (m_new, acc)
        for u in range(n_sub):
            accs = [state[u * n_split + h][1] for h in range(n_split)]
            acc = accs[0] if n_split == 1 else jnp.concatenate(accs, axis=0)
            o = acc[:, :dv] / acc[:, dv:]
            o0 = u * n_heads * dv
            for r in range(n_heads):
                o_ref[rows, o0 + r * dv:o0 + (r + 1) * dv] = o[r * tq:(r + 1) * tq].astype(o_ref.dtype)

    n_ctx_steps = 0 if ctx_rows is None else 1
    i = pl.program_id(2) - n_ctx_steps

    @pl.when(i >= 0)
    def _():
        run(pl.multiple_of(i * tq_lat, tq_lat), tq_lat, lat_chunks)

    if ctx_rows is not None:
        @pl.when(i < 0)
        def _():
            run(ctx_rows[0], ctx_rows[1], ctx_chunks)


def _flash_call(q_arr, k_arr, v_arr, *, name, n_groups, n_sub, n_split, n_heads, dk, dv, tq, k_col0, v_col0, n_lat,
                lat_chunks, ctx_chunks, out_rows):
    bsz, t, _ = k_arr.shape
    assert n_groups % n_sub == 0 and k_col0 % n_sub == 0 and v_col0 % n_sub == 0 and n_lat % tq == 0
    with_ctx = out_rows > n_lat
    kern = functools.partial(_flash_kernel, n_sub=n_sub, n_split=n_split, n_heads=n_heads, dk=dk, dv=dv, tq_lat=tq,
                             lat_chunks=lat_chunks,
                             ctx_rows=(n_lat, out_rows - n_lat) if with_ctx else None, ctx_chunks=ctx_chunks)
    return pl.pallas_call(
        kern,
        grid=(bsz, n_groups // n_sub, n_lat // tq + int(with_ctx)),
        in_specs=[
            pl.BlockSpec((None, t, n_sub * n_heads * dk), lambda b, g, i: (b, 0, g)),
            pl.BlockSpec((None, t, n_sub * dk), lambda b, g, i: (b, 0, k_col0 // n_sub + g)),
            pl.BlockSpec((None, t, n_sub * dv), lambda b, g, i: (b, 0, v_col0 // n_sub + g)),
        ],
        out_specs=pl.BlockSpec((None, out_rows, n_sub * n_heads * dv), lambda b, g, i: (b, 0, g)),
        out_shape=jax.ShapeDtypeStruct((bsz, out_rows, n_groups * n_heads * dv), BF16),
        compiler_params=_cparams(3, VMEM_LIMIT),
        name=name,
    )(q_arr, k_arr, v_arr)


def _na_kernel(q_ref, k_ref, v_ref, bias_ref, o_ref, *, n_lat, n_ctx, n_groups, n_ctx_steps):
    gi = pl.program_id(2) - n_ctx_steps
    rows = n_groups * NA_Q_ROWS
    tq = NA_Q_ROWS * GRID_W
    tk = NA_K_ROWS * GRID_W
    lo = _pair_masks((tq, LANES))

    def stacked_q(cols):
        qb = q_ref[:, cols]
        zero = jnp.zeros_like(qb)
        return jnp.concatenate([jnp.where(lo, qb, zero), jnp.where(lo, zero, qb)], axis=0)

    def with_ones(v):
        return jnp.concatenate([v, jnp.ones(v.shape, v.dtype)], axis=1)

    def finish(o1, cols):
        o = o1[:, :LANES] / o1[:, LANES:]
        o_ref[:, cols] = jnp.where(lo, o[:tq], o[tq:]).astype(o_ref.dtype)

    @pl.when(gi >= 0)
    def _():
        start = jnp.clip(gi * NA_Q_ROWS - B_MAX_KH // 2, 0, rows - NA_K_ROWS) * GRID_W
        start = pl.multiple_of(start, GRID_W)
        scores = []
        for s in range(NA_PAIRS_PER_STEP):
            cols = slice(s * LANES, (s + 1) * LANES)
            kk = jnp.concatenate([k_ref[pl.ds(start, tk), cols], k_ref[n_lat:n_lat + n_ctx, cols]], axis=0)
            scores.append(_dot_nt(stacked_q(cols), kk))
        for s in range(NA_PAIRS_PER_STEP):
            cols = slice(s * LANES, (s + 1) * LANES)
            vv = with_ones(jnp.concatenate([v_ref[pl.ds(start, tk), cols], v_ref[n_lat:n_lat + n_ctx, cols]], axis=0))
            bias = bias_ref[2 * s:2 * s + 2].reshape(2 * tq, tk)
            sc = jnp.concatenate([scores[s][:, :tk] + bias, scores[s][:, tk:]], axis=1)
            p = jnp.exp2(sc - jnp.max(sc, axis=-1, keepdims=True))
            finish(jnp.dot(p.astype(BF16), vv, preferred_element_type=F32), cols)

    @pl.when(gi < 0)
    def _():
        for s in range(NA_PAIRS_PER_STEP):
            cols = slice(s * LANES, (s + 1) * LANES)
            q = stacked_q(cols)
            s_c = _dot_nt(q, k_ref[n_lat:n_lat + n_ctx, cols])
            p_c = jnp.exp2(s_c - jnp.max(s_c, axis=-1, keepdims=True))
            vc = with_ones(v_ref[n_lat:n_lat + n_ctx, cols])
            finish(jnp.dot(p_c.astype(BF16), vc, preferred_element_type=F32), cols)


def _na_call(qkv, bias, n_lat, n_ctx, out_rows):
    bsz, t, w3 = qkv.shape
    d = w3 // 3
    w = NA_PAIRS_PER_STEP * LANES
    n_steps = d // w
    tq = NA_Q_ROWS * GRID_W
    tk = NA_K_ROWS * GRID_W
    n_groups = n_lat // tq
    n_ctx_steps = out_rows // tq - n_groups
    kern = functools.partial(_na_kernel, n_lat=n_lat, n_ctx=n_ctx, n_groups=n_groups, n_ctx_steps=n_ctx_steps)

    def row_tile(g):
        return jnp.where(g < n_ctx_steps, n_groups, g - n_ctx_steps)

    def bias_map(b, p, g):
        gi = g - n_ctx_steps
        case = jnp.where(gi <= 0, 0, jnp.where(gi == n_groups - 1, 2, 1))
        return (p, case, 0, 0)

    assert n_ctx == tq and n_ctx_steps in (0, 1)
    return pl.pallas_call(
        kern,
        grid=(bsz, n_steps, out_rows // tq),
        in_specs=[
            pl.BlockSpec((None, tq, w), lambda b, p, g: (b, row_tile(g), p)),
            pl.BlockSpec((None, t, w), lambda b, p, g: (b, 0, n_steps + p)),
            pl.BlockSpec((None, t, w), lambda b, p, g: (b, 0, 2 * n_steps + p)),
            pl.BlockSpec((2 * NA_PAIRS_PER_STEP, None, tq, tk), bias_map),
        ],
        out_specs=pl.BlockSpec((None, tq, w), lambda b, p, g: (b, row_tile(g), p)),
        out_shape=jax.ShapeDtypeStruct((bsz, out_rows, d), BF16),
        compiler_params=_cparams(3, 48 * 1024 * 1024),
        name="attn_na",
    )(qkv, qkv, qkv, bias)


def _post_kernel(o_ref, *refs, n_x, n_lat_tiles, ctx_row, alpha):
    mod_ref, wo_ref, wg_ref, wu_ref, wd_ref, lng_ref, lnb_ref, out_ref = refs[n_x:]
    x_ref = _Rows(refs[:n_x], n_lat_tiles)
    nb, tm, d = out_ref.shape
    rows = _mod_rows(nb, n_lat_tiles, ctx_row)
    ys = [jnp.dot(o_ref[s], wo_ref[...], preferred_element_type=F32) for s in range(nb)]
    x1s, hs = [], []
    for s, row in enumerate(rows):
        g1 = _mod_vec(mod_ref, row, 2, d)
        sh2 = _mod_vec(mod_ref, row, 3, d)
        sc2 = _mod_vec(mod_ref, row, 4, d)
        x1 = _layer_norm(alpha * x_ref[s] + g1 * ys[s], lng_ref[0:1, :], lnb_ref[0:1, :])
        x1s.append(x1)
        hs.append((x1 * (1.0 + sc2) + sh2).astype(BF16))
    gates = [jnp.dot(h, wg_ref[...], preferred_element_type=F32) for h in hs]
    ups = [jnp.dot(h, wu_ref[...], preferred_element_type=F32) for h in hs]
    acts = [(g / (1.0 + jnp.exp(-g)) * u).astype(BF16) for g, u in zip(gates, ups)]
    fs = [jnp.dot(a, wd_ref[...], preferred_element_type=F32) for a in acts]
    for s, row in enumerate(rows):
        g2 = _mod_vec(mod_ref, row, 5, d)
        out_ref[s] = _layer_norm(alpha * x1s[s] + g2 * fs[s], lng_ref[1:2, :], lnb_ref[1:2, :])


def _post_call(o, xs, mod, layer, wo_all, wo_idx, wg_all, wu_all, wd_all, lng, lnb, n_lat, out_rows, alpha):
    bsz, _, d = xs[0].shape
    tm = ROW_TILE
    d_ff = wg_all.shape[2]
    kern = functools.partial(_post_kernel, n_x=len(xs), n_lat_tiles=n_lat // tm, ctx_row=bsz, alpha=alpha)
    return pl.pallas_call(
        kern,
        grid=(bsz // ROW_BATCHES, out_rows // tm),
        in_specs=[_row_block(ROW_BATCHES, tm, o.shape[-1])]
        + _x_specs(ROW_BATCHES, tm, d, n_lat // tm, len(xs) == 2) + [
            pl.BlockSpec((None, MOD_ROWS, 6 * d), lambda b, i: (layer, 0, 0)),
            _resident((None,) + wo_all.shape[1:], lambda b, i: (wo_idx, 0, 0)),
            _resident((None, d, d_ff), lambda b, i: (layer, 0, 0)),
            _resident((None, d, d_ff), lambda b, i: (layer, 0, 0)),
            _resident((None, d_ff, d), lambda b, i: (layer, 0, 0)),
            pl.BlockSpec((None, 2, d), lambda b, i: (layer, 0, 0)),
            pl.BlockSpec((None, 2, d), lambda b, i: (layer, 0, 0)),
        ],
        out_specs=_row_block(ROW_BATCHES, tm, d),
        out_shape=jax.ShapeDtypeStruct((bsz, out_rows, d), F32),
        compiler_params=_cparams(2, VMEM_LIMIT),
        name="post_ffn",
    )(o, *xs, mod, wo_all, wg_all, wu_all, wd_all, lng, lnb)


def _rope_tables(n_lat, n_ctx, n_freq):
    half = LANES // 2
    t = np.arange(n_lat)
    freqs = ROPE_BASE ** (-np.arange(0, 2 * n_freq, 2, dtype=np.float64) / (2 * n_freq))
    ang = np.concatenate([(t // GRID_W)[:, None] * freqs[None, :], (t % GRID_W)[:, None] * freqs[None, :]], axis=1)
    cos = np.zeros((n_lat + n_ctx, LANES))
    sin = np.zeros((n_lat + n_ctx, LANES))
    w = 2 * n_freq
    cos[:n_lat, :w] = np.cos(ang)
    cos[:n_lat, half:half + w] = np.cos(ang)
    sin[:n_lat, :w] = -np.sin(ang)
    sin[:n_lat, half:half + w] = np.sin(ang)
    cos[n_lat:, :w] = 1.0
    cos[n_lat:, half:half + w] = 1.0
    return jnp.asarray(cos, F32), jnp.asarray(sin, F32)


def _rope_perm(n_freq):
    idx = np.full((LANES,), -1, np.int64)
    f = n_freq
    idx[0:f] = np.arange(0, f)
    idx[f:2 * f] = np.arange(2 * f, 3 * f)
    idx[64:64 + f] = np.arange(f, 2 * f)
    idx[64 + f:64 + 2 * f] = np.arange(3 * f, 4 * f)
    return idx


def _take_cols(w, idx):
    wz = jnp.concatenate([w, jnp.zeros((w.shape[0], 1), w.dtype)], axis=1)
    return wz[:, np.where(idx < 0, w.shape[1], idx)]


def _na_row_cases(rows):
    kh = B_MAX_KH
    i = np.arange(NA_Q_ROWS)[:, None]
    j = np.arange(NA_K_ROWS)[None, :]
    cases = []
    for r0 in (0, NA_Q_ROWS, rows - NA_Q_ROWS):
        start = int(np.clip(r0 - kh // 2, 0, rows - NA_K_ROWS))
        rs = np.clip(r0 + i - kh // 2, 0, rows - kh)
        key_row = start + j
        cases.append(((key_row >= rs) & (key_row < rs + kh), key_row - (r0 + i) + (kh - 1)))
    return np.stack([c[0] for c in cases]), np.stack([c[1] for c in cases])


def _na_bias_kernel(r2_ref, o_ref, *, row_ok, dr_idx):
    n_e = r2_ref.shape[0]
    shape = (GRID_W, LANES)
    c = lax.broadcasted_iota(jnp.int32, shape, 0)
    lane = lax.broadcasted_iota(jnp.int32, shape, 1)
    kc = jnp.where(lane < GRID_W, lane, lane - GRID_W)
    cs = jnp.clip(c - B_KW // 2, 0, GRID_W - B_KW)
    col_ok = (kc >= cs) & (kc < cs + B_KW)
    left = lane < GRID_W
    neg = jnp.full(shape, NEG_BIG, F32)
    x = r2_ref[...]
    tiles = []
    for e in range(n_e):
        t = pltpu.roll(jnp.broadcast_to(x[e:e + 1, :], shape), LANES - (B_KW - 1), 1, stride=1, stride_axis=0)
        tiles.append(jnp.where(col_ok, t * LOG2E, neg))
    for case in range(row_ok.shape[0]):
        for i in range(NA_Q_ROWS):
            for jp in range(NA_K_ROWS // 2):
                ok_l, ok_r = bool(row_ok[case, i, 2 * jp]), bool(row_ok[case, i, 2 * jp + 1])
                e = int(np.clip(dr_idx[case, i, 2 * jp] + 1, 0, n_e - 1))
                if ok_l and ok_r:
                    tile = tiles[e]
                elif ok_l:
                    tile = jnp.where(left, tiles[e], neg)
                elif ok_r:
                    tile = jnp.where(left, neg, tiles[e])
                else:
                    tile = neg
                o_ref[case, i * GRID_W:(i + 1) * GRID_W, jp * LANES:(jp + 1) * LANES] = tile


def _na_bias_tables(rpb, rows):
    n_heads, n_dr, n_dc = rpb.shape
    n_e = n_dr + 1
    r2 = jnp.zeros((n_heads, n_e, LANES), F32)
    r2 = r2.at[:, 1:, :n_dc].set(rpb).at[:, :n_dr, GRID_W:GRID_W + n_dc].set(rpb)
    row_ok, dr_idx = _na_row_cases(rows)
    kern = functools.partial(_na_bias_kernel, row_ok=row_ok, dr_idx=dr_idx)
    tq, tk = NA_Q_ROWS * GRID_W, NA_K_ROWS * GRID_W
    return pl.pallas_call(
        kern,
        grid=(n_heads,),
        in_specs=[pl.BlockSpec((None, n_e, LANES), lambda h: (h, 0, 0))],
        out_specs=pl.BlockSpec((None, 3, tq, tk), lambda h: (h, 0, 0, 0)),
        out_shape=jax.ShapeDtypeStruct((n_heads, 3, tq, tk), F32),
        compiler_params=_cparams(1, 32 * 1024 * 1024),
        name="na_bias",
    )(r2)


def _combined(xs):
    return xs[0] if len(xs) == 1 else jnp.concatenate(xs, axis=1)


def kernel(x, c, ctx, c_ctx, w_ada, b_ada, ln_g, ln_b, w_ffn_gate, w_ffn_up, w_ffn_down, a_w_qkv, a_q_gain, a_k_gain, a_w_o, b_w_qkv, b_rpb, b_w_o, c_w_dqkv, c_q_a_gain, c_kv_a_gain, c_w_uq, c_w_ukv, c_w_o):
    bsz, n_lat, d = x.shape
    n_ctx = ctx.shape[1]
    t = n_lat + n_ctx
    depth = w_ada.shape[0]
    rows = n_lat // GRID_W
    assert n_lat % ROW_TILE == 0 and n_ctx == ROW_TILE and bsz < MOD_ROWS and bsz % ROW_BATCHES == 0
    assert n_lat % (NA_Q_ROWS * GRID_W) == 0 and rows >= NA_K_ROWS
    assert a_w_qkv.shape[2] == (A_HEADS + 2 * A_KV_HEADS) * A_HEAD_DIM and b_rpb.shape[1] == B_HEADS
    assert c_w_dqkv.shape[2] == C_Q_RANK + C_KV_RANK + C_ROPE and d == B_HEADS * B_HEAD_DIM
    alpha = (2.0 * depth) ** 0.25

    xs = (x, ctx)
    cond = jnp.zeros((MOD_ROWS, d), F32).at[:bsz].set(c).at[bsz].set(c_ctx)
    mod = _ada_call(cond, w_ada, b_ada)

    cos_a, sin_a = _rope_tables(n_lat, n_ctx, A_HEAD_DIM // 4)
    cos_c, sin_c = _rope_tables(n_lat, n_ctx, C_ROPE // 4)
    perm_a = _rope_perm(A_HEAD_DIM // 4)
    perm_c = _rope_perm(C_ROPE // 4)

    wg_all, wu_all, wd_all = w_ffn_gate.astype(BF16), w_ffn_up.astype(BF16), w_ffn_down.astype(BF16)
    wo_a, wo_b, wo_c = a_w_o.astype(BF16), b_w_o.astype(BF16), c_w_o.astype(BF16)

    ctx_chunks = ((n_lat, n_ctx),)
    lat_chunks = ((0, 2048), (2048, t - 2048))

    for i in range(depth):
        last = i == depth - 1
        out_rows = n_lat if last else t
        kind, j = i % 3, i // 3
        if kind == 0:
            hd = A_HEAD_DIM
            nqk = A_HEADS + A_KV_HEADS
            col_idx = np.concatenate([h * hd + perm_a for h in range(nqk)]
                                     + [np.arange(nqk * hd, (nqk + A_KV_HEADS) * hd)])
            w = a_w_qkv[j][:, col_idx].astype(BF16)
            qkv = _qkv_a_call(xs, mod, i, w, a_q_gain[j][perm_a][None, :], a_k_gain[j][perm_a][None, :],
                              cos_a, sin_a, n_lat, t)
            o = _flash_call(qkv, qkv, qkv, name="attn_gqa", n_groups=A_KV_HEADS, n_sub=1, n_split=2,
                            n_heads=A_HEADS // A_KV_HEADS, dk=hd, dv=hd, tq=256, k_col0=A_HEADS,
                            v_col0=A_HEADS + A_KV_HEADS, n_lat=n_lat, lat_chunks=lat_chunks,
                            ctx_chunks=ctx_chunks, out_rows=out_rows)
            wo_all = wo_a
        elif kind == 1:
            xs = (_combined(xs),)
            qkv = _qkv_b_call(xs[0], mod, i, b_w_qkv[j].astype(BF16), n_lat)
            o = _na_call(qkv, _na_bias_tables(b_rpb[j], rows), n_lat, n_ctx, out_rows)
            wo_all = wo_b
        else:
            r1 = C_Q_RANK + C_KV_RANK
            wd_idx = np.concatenate([np.arange(r1), np.where(perm_c < 0, -1, r1 + perm_c)])
            wdn = _take_cols(c_w_dqkv[j], wd_idx).astype(BF16)
            hq = C_NOPE + C_ROPE
            uq_idx = np.concatenate([np.concatenate([h * hq + np.arange(C_NOPE),
                                                     np.where(perm_c < 0, -1, h * hq + C_NOPE + perm_c)])
                                     for h in range(C_HEADS)])
            wuq = _take_cols(c_w_uq[j], uq_idx).astype(BF16)
            xs = (_combined(xs),)
            qc, kc, vc = _proj_c_call(xs[0], mod, i, wdn, c_q_a_gain[j][None, :], c_kv_a_gain[j][None, :], wuq,
                                      c_w_ukv[j].astype(BF16), cos_c, sin_c, n_lat)
            o = _flash_call(qc, kc, vc, name="attn_mla", n_groups=C_HEADS, n_sub=1, n_split=2, n_heads=1, dk=2 * LANES,
                            dv=C_V, tq=1024, k_col0=0, v_col0=0, n_lat=n_lat, lat_chunks=lat_chunks,
                            ctx_chunks=ctx_chunks, out_rows=out_rows)
            wo_all = wo_c
        xs = (_post_call(o, xs, mod, i, wo_all, j, wg_all, wu_all, wd_all, ln_g, ln_b, n_lat, out_rows, alpha),)
    return xs[0]
```
